```python
import numpy as np
import jax, jax.numpy as jnp
from jax import lax

D_MODEL = 1024
BATCH = 4
SEQ = 4096
DEPTH = 4
DEC_BATCH = 128
DEC_SEQ = 8
PAST_LEN = 2048
PAGE_SIZE = 128

HEAD_DIM = 64
SB_HEADS = 8
SB_WIDTH = SB_HEADS * HEAD_DIM
NSA_HEADS = 8
NSA_KV_HEADS = 2
NSA_GROUP = NSA_HEADS // NSA_KV_HEADS
NSA_WIDTH = NSA_HEADS * HEAD_DIM
NSA_KV_WIDTH = NSA_KV_HEADS * HEAD_DIM
N_NSA_BRANCH = 3
CMP_BLOCK = 32
CMP_STRIDE = 16
SEL_BLOCK = 64
SEL_TOPK = 8
WINDOW = 512
Q_BLOCK = 128
LN_EPS = 1e-5
DEEPNORM_ALPHA = (2 * DEPTH) ** 0.25
DEEPNORM_BETA = (8 * DEPTH) ** -0.25

KV_CH = 2 * SB_WIDTH + 4 * NSA_KV_WIDTH
WIN_CH = 2 * NSA_KV_WIDTH
OFF_WIN = KV_CH
OFF_SB_Q = OFF_WIN + WIN_CH
OFF_NSA_Q = OFF_SB_Q + SB_WIDTH
OFF_SB_GATE = OFF_NSA_Q + NSA_WIDTH
OFF_NSA_GATE = OFF_SB_GATE + SB_WIDTH
OFF_BR_GATE = OFF_NSA_GATE + NSA_WIDTH
OFF_MERGE = OFF_BR_GATE + N_NSA_BRANCH * NSA_HEADS
IN_COLS = OFF_MERGE + 2 * D_MODEL

kernel_name = "stickbreak_nsa_hybrid_step"


def layer_norm(x, g, b):
    xf = x.astype(jnp.float32)
    mu = xf.mean(-1, keepdims=True)
    var = jnp.square(xf - mu).mean(-1, keepdims=True)
    return ((xf - mu) * lax.rsqrt(var + LN_EPS) * g + b).astype(x.dtype)


def masked_softmax(s, mask):
    s = jnp.where(mask, s, -jnp.inf)
    m = jnp.max(s, axis=-1, keepdims=True)
    m = jnp.where(jnp.isfinite(m), m, 0.0)
    e = jnp.where(mask, jnp.exp(s - m), 0.0)
    return e / jnp.maximum(e.sum(-1, keepdims=True), 1e-30)


def alibi_slopes():
    h = jnp.arange(1, NSA_HEADS + 1, dtype=jnp.float32)
    return jnp.exp2(-8.0 * h / NSA_HEADS).reshape(NSA_KV_HEADS, NSA_GROUP)


def split_kv(rows):
    lead = rows.shape[:-1]
    sb_k = rows[..., :SB_WIDTH].reshape(lead + (SB_HEADS, HEAD_DIM))
    sb_v = rows[..., SB_WIDTH:2 * SB_WIDTH].reshape(lead + (SB_HEADS, HEAD_DIM))
    o = 2 * SB_WIDTH
    cmp_k, cmp_v, slc_k, slc_v = [
        rows[..., o + i * NSA_KV_WIDTH:o + (i + 1) * NSA_KV_WIDTH].reshape(lead + (NSA_KV_HEADS, HEAD_DIM))
        for i in range(4)]
    return sb_k, sb_v, cmp_k, cmp_v, slc_k, slc_v


def split_win(rows):
    lead = rows.shape[:-1]
    win_k = rows[..., :NSA_KV_WIDTH].reshape(lead + (NSA_KV_HEADS, HEAD_DIM))
    win_v = rows[..., NSA_KV_WIDTH:2 * NSA_KV_WIDTH].reshape(lead + (NSA_KV_HEADS, HEAD_DIM))
    return win_k, win_v


def split_queries(proj):
    lead = proj.shape[:-1]
    sb_q = proj[..., OFF_SB_Q:OFF_SB_Q + SB_WIDTH].reshape(lead + (SB_HEADS, HEAD_DIM))
    nsa_q = proj[..., OFF_NSA_Q:OFF_NSA_Q + NSA_WIDTH].reshape(lead + (NSA_KV_HEADS, NSA_GROUP, HEAD_DIM))
    br_gate = jax.nn.sigmoid(proj[..., OFF_BR_GATE:OFF_BR_GATE + N_NSA_BRANCH * NSA_HEADS]).reshape(
        lead + (NSA_KV_HEADS, NSA_GROUP, N_NSA_BRANCH))
    return sb_q, nsa_q, br_gate


def stick_breaking(q, k, v, q_pos, k_pos):
    z = jnp.einsum('bqhd,bkhd->bhqk', q, k).astype(jnp.float32) * (HEAD_DIM ** -0.5)
    causal = k_pos[None, :] < q_pos[:, None]
    log_stay = jnp.where(causal, jax.nn.log_sigmoid(-z), 0.0)
    log_rest = lax.cumsum(log_stay, axis=3, reverse=True) - log_stay
    a = jnp.where(causal, jnp.exp(jax.nn.log_sigmoid(z) + log_rest), 0.0)
    return jnp.einsum('bhqk,bkhd->bqhd', a.astype(v.dtype), v)


def compress(k, w):
    B, L = k.shape[:2]
    r = CMP_BLOCK // CMP_STRIDE
    halves = k.reshape(B, L // CMP_STRIDE, CMP_STRIDE, NSA_KV_HEADS, HEAD_DIM)
    n = L // CMP_STRIDE - r + 1
    out = jnp.einsum('bnpgd,p->bngd', halves[:, :n], w[:CMP_STRIDE].astype(k.dtype))
    for i in range(1, r):
        out = out + jnp.einsum('bnpgd,p->bngd', halves[:, i:i + n],
                               w[i * CMP_STRIDE:(i + 1) * CMP_STRIDE].astype(k.dtype))
    end = jnp.arange(n, dtype=jnp.int32) * CMP_STRIDE + (CMP_BLOCK - 1)
    return out, end


def sel_blocks(a):
    B, L = a.shape[:2]
    return a.reshape(B, L // SEL_BLOCK, SEL_BLOCK, NSA_KV_HEADS, HEAD_DIM).transpose(0, 3, 1, 2, 4)


def nsa_block(q, br_gate, q_pos, kc, vc, c_end, sk, sv, kw, vw, w_pos):
    B, Tq = q.shape[:2]
    scale = HEAD_DIM ** -0.5
    slopes = alibi_slopes()
    s = jnp.einsum('bqgrd,bngd->bgrqn', q, kc).astype(jnp.float32) * scale
    s = s - slopes[:, :, None, None] * (q_pos[:, None] - c_end[None, :]).astype(jnp.float32)
    p_cmp = masked_softmax(s, c_end[None, :] <= q_pos[:, None])
    o_cmp = jnp.einsum('bgrqn,bngd->bqgrd', p_cmp.astype(vc.dtype), vc)
    ns = sk.shape[2]
    ratio = SEL_BLOCK // CMP_STRIDE
    taps = np.convolve(np.ones(ratio), np.ones(CMP_BLOCK // CMP_STRIDE))
    imp = p_cmp.sum(axis=2)
    imp = jnp.pad(imp, ((0, 0), (0, 0), (0, 0), (0, ratio * ns + len(taps) - imp.shape[-1])))
    p_slc = float(taps[0]) * imp[..., 0:ratio * ns:ratio]
    for i in range(1, len(taps)):
        p_slc = p_slc + float(taps[i]) * imp[..., i:i + ratio * ns:ratio]
    blk = jnp.arange(ns, dtype=jnp.int32)
    cur = (q_pos // SEL_BLOCK)[:, None]
    forced = (blk == 0) | (blk == cur) | (blk == cur - 1)
    score = jnp.where(blk > cur, -jnp.inf, jnp.where(forced, jnp.inf, p_slc))
    _, idx = lax.top_k(score, min(SEL_TOPK, ns))
    bi = jnp.arange(B)[:, None, None, None]
    gi = jnp.arange(NSA_KV_HEADS)[None, :, None, None]
    k_sel = sk[bi, gi, idx]
    v_sel = sv[bi, gi, idx]
    s_pos = idx[..., None] * SEL_BLOCK + jnp.arange(SEL_BLOCK, dtype=jnp.int32)
    dist = q_pos[:, None, None] - s_pos
    s = jnp.einsum('bqgrd,bgqkpd->bgrqkp', q, k_sel).astype(jnp.float32) * scale
    s = s - slopes[None, :, :, None, None, None] * dist[:, :, None].astype(jnp.float32)
    kk, pp = s.shape[4], s.shape[5]
    p_sel = masked_softmax(s.reshape(s.shape[:4] + (kk * pp,)),
                           (dist >= 0)[:, :, None].reshape(B, NSA_KV_HEADS, 1, Tq, kk * pp))
    o_slc = jnp.einsum('bgrqkp,bgqkpd->bqgrd', p_sel.reshape(s.shape).astype(v_sel.dtype), v_sel)
    s = jnp.einsum('bqgrd,bkgd->bgrqk', q, kw).astype(jnp.float32) * scale
    wd = q_pos[:, None] - w_pos[None, :]
    s = s - slopes[:, :, None, None] * wd.astype(jnp.float32)
    p_win = masked_softmax(s, (wd >= 0) & (wd <= WINDOW) & (w_pos[None, :] >= 0))
    o_win = jnp.einsum('bgrqk,bkgd->bqgrd', p_win.astype(vw.dtype), vw)
    g = br_gate[..., None]
    return g[..., 0, :] * o_cmp + g[..., 1, :] * o_slc + g[..., 2, :] * o_win


def merge_branches(x, proj, o_sb, o_nsa, w_sb, w_nsa, w_out, b_out, ln_g, ln_b):
    gate_sb = jax.nn.sigmoid(proj[..., OFF_MERGE:OFF_MERGE + D_MODEL])
    gate_nsa = jax.nn.sigmoid(proj[..., OFF_MERGE + D_MODEL:OFF_MERGE + 2 * D_MODEL])
    u_sb = (o_sb * jax.nn.silu(proj[..., OFF_SB_GATE:OFF_SB_GATE + SB_WIDTH])) @ w_sb
    u_nsa = (o_nsa * jax.nn.silu(proj[..., OFF_NSA_GATE:OFF_NSA_GATE + NSA_WIDTH])) @ w_nsa
    h = (gate_sb * u_sb + gate_nsa * u_nsa) @ w_out + b_out
    return layer_norm(DEEPNORM_ALPHA * x + h, ln_g, ln_b)


def prompt_layer(x, w_in, b_in, cmp_w, w_sb, w_nsa, w_out, b_out, ln_g, ln_b):
    B, T, _ = x.shape
    proj = x @ w_in + b_in
    kv_rows = proj[..., :KV_CH]
    win_rows = proj[..., OFF_WIN:OFF_WIN + WIN_CH]
    sb_q, nsa_q, br_gate = split_queries(proj)
    sb_k, sb_v, cmp_k, cmp_v, slc_k, slc_v = split_kv(kv_rows)
    win_k, win_v = split_win(win_rows)
    nb = T // Q_BLOCK
    k_pos = jnp.arange(T, dtype=jnp.int32)
    blk_ids = jnp.arange(nb, dtype=jnp.int32)

    def to_blocks(a):
        return jnp.moveaxis(a.reshape((B, nb, Q_BLOCK) + a.shape[2:]), 1, 0)

    def from_blocks(a):
        return jnp.moveaxis(a, 0, 1).reshape((B, T) + a.shape[3:])

    def sb_step(args):
        qb, i = args
        return stick_breaking(qb, sb_k, sb_v, i * Q_BLOCK + jnp.arange(Q_BLOCK, dtype=jnp.int32), k_pos)

    o_sb = from_blocks(lax.map(sb_step, (to_blocks(sb_q), blk_ids)))

    kc, c_end = compress(cmp_k, cmp_w[0])
    vc, _ = compress(cmp_v, cmp_w[1])
    sk, sv = sel_blocks(slc_k), sel_blocks(slc_v)
    front = ((0, 0), (WINDOW, 0), (0, 0), (0, 0))
    kw_pad, vw_pad = jnp.pad(win_k, front), jnp.pad(win_v, front)
    w_len = WINDOW + Q_BLOCK

    def nsa_step(args):
        qb, gb, i = args
        start = i * Q_BLOCK
        q_pos = start + jnp.arange(Q_BLOCK, dtype=jnp.int32)
        kwb = lax.dynamic_slice_in_dim(kw_pad, start, w_len, axis=1)
        vwb = lax.dynamic_slice_in_dim(vw_pad, start, w_len, axis=1)
        w_pos = start - WINDOW + jnp.arange(w_len, dtype=jnp.int32)
        return nsa_block(qb, gb, q_pos, kc, vc, c_end, sk, sv, kwb, vwb, w_pos)

    o_nsa = from_blocks(lax.map(nsa_step, (to_blocks(nsa_q), to_blocks(br_gate), blk_ids)))
    y = merge_branches(x, proj, o_sb.reshape(B, T, SB_WIDTH), o_nsa.reshape(B, T, NSA_WIDTH),
                       w_sb, w_nsa, w_out, b_out, ln_g, ln_b)
    return y, kv_rows, win_rows[:, -min(WINDOW, T):]


def sample_layer(x, kv_past, win_buf, w_in, b_in, cmp_w, w_sb, w_nsa, w_out, b_out, ln_g, ln_b):
    Bd, Tn, _ = x.shape
    P = kv_past.shape[1]
    proj = x @ w_in + b_in
    kv_new = proj[..., :KV_CH]
    win_new = proj[..., OFF_WIN:OFF_WIN + WIN_CH]
    sb_q, nsa_q, br_gate = split_queries(proj)
    L = P + Tn
    sb_k, sb_v, cmp_k, cmp_v, slc_k, slc_v = split_kv(jnp.concatenate([kv_past, kv_new], axis=1))
    q_pos = P + jnp.arange(Tn, dtype=jnp.int32)
    o_sb = stick_breaking(sb_q, sb_k, sb_v, q_pos, jnp.arange(L, dtype=jnp.int32))
    l_pad = -(-L // SEL_BLOCK) * SEL_BLOCK
    back = ((0, 0), (0, l_pad - L), (0, 0), (0, 0))
    kc, c_end = compress(jnp.pad(cmp_k, back), cmp_w[0])
    vc, _ = compress(jnp.pad(cmp_v, back), cmp_w[1])
    sk, sv = sel_blocks(jnp.pad(slc_k, back)), sel_blocks(jnp.pad(slc_v, back))
    n_buf = win_buf.shape[1]
    win_all = jnp.concatenate([win_buf, win_new], axis=1)
    win_k, win_v = split_win(win_all)
    w_pos = P - n_buf + jnp.arange(n_buf + Tn, dtype=jnp.int32)
    o_nsa = nsa_block(nsa_q, br_gate, q_pos, kc, vc, c_end, sk, sv, win_k, win_v, w_pos)
    y = merge_branches(x, proj, o_sb.reshape(Bd, Tn, SB_WIDTH), o_nsa.reshape(Bd, Tn, NSA_WIDTH),
                       w_sb, w_nsa, w_out, b_out, ln_g, ln_b)
    return y, kv_new, win_all[:, -n_buf:]


def setup_inputs(seed: int = 0) -> dict:
    key = jax.random.key(seed)
    ks = jax.random.split(key, 16)
    f32 = jnp.float32
    n_pages = PAST_LEN // PAGE_SIZE
    n_phys = (5 * DEC_BATCH * n_pages) // 4
    win_buf = min(WINDOW, PAST_LEN)
    x_prompt = jax.random.normal(ks[0], (BATCH, SEQ, D_MODEL), f32)
    x_sample = jax.random.normal(ks[1], (DEC_BATCH, DEC_SEQ, D_MODEL), f32)
    cache_kv = jax.random.normal(ks[2], (DEPTH, n_phys, PAGE_SIZE, KV_CH), f32)
    state_win = jax.random.normal(ks[3], (DEPTH, DEC_BATCH, win_buf, WIN_CH), f32)
    page_table = jax.random.permutation(ks[4], n_phys)[:DEC_BATCH * n_pages].reshape(
        DEC_BATCH, n_pages).astype(jnp.int32)
    w_in = jax.random.normal(ks[5], (DEPTH, D_MODEL, IN_COLS), f32) * D_MODEL ** -0.5
    b_in = 0.02 * jax.random.normal(ks[6], (DEPTH, IN_COLS), f32)
    cmp_w = (1.0 + 0.5 * jax.random.normal(ks[7], (DEPTH, 2, CMP_BLOCK), f32)) * CMP_BLOCK ** -0.5
    w_sb = jax.random.normal(ks[8], (DEPTH, SB_WIDTH, D_MODEL), f32) * (SB_WIDTH ** -0.5 * DEEPNORM_BETA)
    w_nsa = jax.random.normal(ks[9], (DEPTH, NSA_WIDTH, D_MODEL), f32) * (NSA_WIDTH ** -0.5 * DEEPNORM_BETA)
    w_out = jax.random.normal(ks[10], (DEPTH, D_MODEL, D_MODEL), f32) * (D_MODEL ** -0.5 * DEEPNORM_BETA)
    b_out = 0.02 * jax.random.normal(ks[11], (DEPTH, D_MODEL), f32)
    ln_g = 1.0 + 0.02 * jax.random.normal(ks[12], (DEPTH, D_MODEL), f32)
    ln_b = 0.02 * jax.random.normal(ks[13], (DEPTH, D_MODEL), f32)
    return {"x_prompt": x_prompt, "x_sample": x_sample, "cache_kv": cache_kv, "state_win": state_win,
            "page_table": page_table, "w_in": w_in, "b_in": b_in, "cmp_w": cmp_w, "w_sb": w_sb,
            "w_nsa": w_nsa, "w_out": w_out, "b_out": b_out, "ln_g": ln_g, "ln_b": ln_b}


def reference(x_prompt, x_sample, cache_kv, state_win, page_table, w_in, b_in, cmp_w, w_sb, w_nsa,
              w_out, b_out, ln_g, ln_b):
    n_seq, n_pages = page_table.shape
    xp, xs = x_prompt, x_sample
    kv_p, kv_s, win_p, win_s = [], [], [], []
    for l in range(DEPTH):
        params = (w_in[l], b_in[l], cmp_w[l], w_sb[l], w_nsa[l], w_out[l], b_out[l], ln_g[l], ln_b[l])
        xp, kvp, wp = prompt_layer(xp, *params)
        past = cache_kv[l, page_table].reshape(n_seq, n_pages * PAGE_SIZE, KV_CH)
        xs, kvs, ws = sample_layer(xs, past, state_win[l], *params)
        kv_p.append(kvp)
        kv_s.append(kvs)
        win_p.append(wp)
        win_s.append(ws)
    return (xp, xs, jnp.stack(kv_p), jnp.stack(kv_s), jnp.stack(win_p), jnp.stack(win_s))
```

```python
import functools

import numpy as np
import jax
import jax.numpy as jnp
from jax import lax
from jax.experimental import pallas as pl
from jax.experimental.pallas import tpu as pltpu

F32 = jnp.float32
BF16 = jnp.bfloat16

D_MODEL = 1024
HEAD_DIM = 64
SB_HEADS = 8
SB_WIDTH = SB_HEADS * HEAD_DIM
NSA_HEADS = 8
NSA_KV_HEADS = 2
NSA_GROUP = NSA_HEADS // NSA_KV_HEADS
NSA_WIDTH = NSA_HEADS * HEAD_DIM
NSA_KV_WIDTH = NSA_KV_HEADS * HEAD_DIM
N_NSA_BRANCH = 3
CMP_BLOCK = 32
CMP_STRIDE = 16
SEL_BLOCK = 64
SEL_TOPK = 8
WINDOW = 512
PAGE_SIZE = 128
LN_EPS = 1e-5

KV_CH = 2 * SB_WIDTH + 4 * NSA_KV_WIDTH
WIN_CH = 2 * NSA_KV_WIDTH
OFF_WIN = KV_CH
OFF_SB_Q = OFF_WIN + WIN_CH
OFF_NSA_Q = OFF_SB_Q + SB_WIDTH
OFF_SB_GATE = OFF_NSA_Q + NSA_WIDTH
OFF_NSA_GATE = OFF_SB_GATE + SB_WIDTH
OFF_BR_GATE = OFF_NSA_GATE + NSA_WIDTH
OFF_MERGE = OFF_BR_GATE + N_NSA_BRANCH * NSA_HEADS
IN_COLS = OFF_MERGE + 2 * D_MODEL

LANES = 128
Q_PLACED = SB_HEADS * LANES
GATE_COLS = SB_WIDTH + NSA_WIDTH + 2 * D_MODEL + LANES
GATE_BR_BLOCK = (SB_WIDTH + NSA_WIDTH + 2 * D_MODEL) // LANES
SCALE = HEAD_DIM ** -0.5
NEG = -1e30
VMEM_LIMIT = 56 * 1024 * 1024

_NT = (((1,), (1,)), ((), ()))


def _slope(h):
    return float(2.0 ** (-8.0 * (h + 1) / NSA_HEADS))


def _dot(a, b):
    return jnp.dot(a, b, preferred_element_type=F32)


def _dot_nt(a, b):
    return lax.dot_general(a, b, _NT, preferred_element_type=F32)


def _split2(x):
    hi = x.astype(BF16)
    lo = (x - hi.astype(F32)).astype(BF16)
    return hi, lo


def _split3(x):
    hi = x.astype(BF16)
    r = x - hi.astype(F32)
    mid = r.astype(BF16)
    lo = (r - mid.astype(F32)).astype(BF16)
    return hi, mid, lo


def _softplus(z):
    return jnp.maximum(z, 0.0) + jnp.log1p(jnp.exp(-jnp.abs(z)))


def _sigmoid(x):
    return 1.0 / (1.0 + jnp.exp(-x))


def _masked_softmax(s, mask):
    s = jnp.where(mask, s, NEG)
    m = jnp.max(s, axis=-1, keepdims=True)
    e = jnp.where(mask, jnp.exp(s - m), 0.0)
    return e / jnp.maximum(jnp.sum(e, axis=-1, keepdims=True), 1e-30)


def _linear_kernel(x_ref, w_ref, b_ref, *out_refs):
    acc = _dot(x_ref[...], w_ref[...]) + b_ref[...]
    for o in out_refs:
        o[...] = acc.astype(o.dtype)


def _linear(x, w, b, out_dtypes, tm, tn):
    n, k = x.shape
    c = w.shape[1]
    tm = min(tm, n)
    assert n % tm == 0 and c % tn == 0
    return pl.pallas_call(
        _linear_kernel,
        grid=(n // tm, c // tn),
        in_specs=[pl.BlockSpec((tm, k), lambda i, j: (i, 0)),
                  pl.BlockSpec((k, tn), lambda i, j: (0, j)),
                  pl.BlockSpec((1, tn), lambda i, j: (0, j))],
        out_specs=[pl.BlockSpec((tm, tn), lambda i, j: (i, j)) for _ in out_dtypes],
        out_shape=[jax.ShapeDtypeStruct((n, c), dt) for dt in out_dtypes],
        compiler_params=pltpu.CompilerParams(dimension_semantics=("parallel", "parallel"),
                                             vmem_limit_bytes=VMEM_LIMIT),
        name="linear",
    )(x, w, b)


def _sb_block(q, k, v, u, carry, diag_mask):
    z = _dot_nt(q, k)
    ls = -_softplus(z)
    if diag_mask is not None:
        ls = jnp.where(diag_mask, ls, 0.0)
    hi, lo = _split2(ls)
    rest = _dot(hi, u) + _dot(lo, u)
    a = jnp.exp(z + ls + rest + carry)
    if diag_mask is not None:
        a = jnp.where(diag_mask, a, 0.0)
    return _dot(a.astype(BF16), v), jnp.sum(ls, axis=1, keepdims=True)


def _sb_prompt_kernel(q_ref, k_ref, v_ref, g_ref, u_ref, o_ref, acc_ref, car_ref, *, tq):
    i = pl.program_id(2)
    u = u_ref[...]
    row = lax.broadcasted_iota(jnp.int32, (tq, tq), 0)
    col = lax.broadcasted_iota(jnp.int32, (tq, tq), 1)
    diag = col < row
    reps = tq // LANES
    for hh in range(2):
        q = q_ref[0, :, hh * LANES:(hh + 1) * LANES]
        start = pl.multiple_of(i * tq, tq)
        av, tot = _sb_block(q, k_ref[0, pl.ds(start, tq), :], v_ref[0, pl.ds(start, tq), :], u,
                            jnp.zeros((tq, tq), F32), diag)
        acc_ref[hh] = av
        car_ref[hh] = jnp.broadcast_to(tot, (tq, LANES))

        def body(s, c, q=q, hh=hh):
            st = pl.multiple_of((i - s) * tq, tq)
            c1 = car_ref[hh]
            av, tot = _sb_block(q, k_ref[0, pl.ds(st, tq), :], v_ref[0, pl.ds(st, tq), :], u,
                                jnp.concatenate([c1] * reps, axis=1), None)
            acc_ref[hh] += av
            car_ref[hh] = c1 + tot
            return c

        lax.fori_loop(1, i + 1, body, 0)
    lane = lax.broadcasted_iota(jnp.int32, (tq, LANES), 1)
    o = jnp.where(lane < HEAD_DIM, acc_ref[0], acc_ref[1])
    g = g_ref[0]
    o_ref[0] = (o * (g * _sigmoid(g))).astype(o_ref.dtype)


def _upper_ones(t):
    r = np.arange(t)
    return jnp.asarray((r[:, None] > r[None, :]).astype(np.float32), dtype=BF16)


def _sb_prompt(qp, kvb, gates, tq):
    b, t, _ = qp.shape
    tq = min(tq, t)
    hp = SB_HEADS // 2
    return pl.pallas_call(
        functools.partial(_sb_prompt_kernel, tq=tq),
        grid=(b, hp, t // tq),
        in_specs=[pl.BlockSpec((1, tq, 2 * LANES), lambda bi, h, i: (bi, i, h)),
                  pl.BlockSpec((1, t, LANES), lambda bi, h, i: (bi, 0, h)),
                  pl.BlockSpec((1, t, LANES), lambda bi, h, i: (bi, 0, hp + h)),
                  pl.BlockSpec((1, tq, LANES), lambda bi, h, i: (bi, i, h)),
                  pl.BlockSpec((tq, tq), lambda bi, h, i: (0, 0))],
        out_specs=pl.BlockSpec((1, tq, LANES), lambda bi, h, i: (bi, i, h)),
        out_shape=jax.ShapeDtypeStruct((b, t, SB_WIDTH), BF16),
        scratch_shapes=[pltpu.VMEM((2, tq, LANES), F32), pltpu.VMEM((2, tq, LANES), F32)],
        compiler_params=pltpu.CompilerParams(dimension_semantics=("parallel", "parallel", "arbitrary"),
                                             vmem_limit_bytes=VMEM_LIMIT),
        name="sb_prompt",
    )(qp, kvb, kvb, gates, _upper_ones(tq))


def _compress_kernel(k_ref, v_ref, w_ref, o_ref, sh_ref, *, ng):
    for c, x_ref in enumerate((k_ref, v_ref)):
        a = jnp.zeros((ng, LANES), F32)
        bh = jnp.zeros((ng, LANES), F32)
        for p in range(CMP_STRIDE):
            xp = x_ref[0, pl.ds(p, ng, stride=CMP_STRIDE), :]
            a = a + xp * w_ref[c, p:p + 1, :]
            bh = bh + xp * w_ref[c, CMP_STRIDE + p:CMP_STRIDE + p + 1, :]
        sh_ref[0:ng, :] = bh
        sh_ref[ng:ng + 8, :] = jnp.zeros((8, LANES), F32)
        o_ref[0, :, c * LANES:(c + 1) * LANES] = (a + sh_ref[1:ng + 1, :]).astype(o_ref.dtype)


def _cmp_plane(cmp_w_l):
    return jnp.broadcast_to(cmp_w_l[:, :, None], (2, CMP_BLOCK, LANES))


def _compress(kv, wplane):
    b, l, _ = kv.shape
    assert CMP_BLOCK == 2 * CMP_STRIDE
    ng = l // CMP_STRIDE
    cmp_k_blk = 2 * SB_WIDTH // LANES
    return pl.pallas_call(
        functools.partial(_compress_kernel, ng=ng),
        grid=(b,),
        in_specs=[pl.BlockSpec((1, l, LANES), lambda bi: (bi, 0, cmp_k_blk)),
                  pl.BlockSpec((1, l, LANES), lambda bi: (bi, 0, cmp_k_blk + 1)),
                  pl.BlockSpec((2, CMP_BLOCK, LANES), lambda bi: (0, 0, 0))],
        out_specs=pl.BlockSpec((1, ng, 2 * LANES), lambda bi: (bi, 0, 0)),
        out_shape=jax.ShapeDtypeStruct((b, ng, 2 * LANES), BF16),
        scratch_shapes=[pltpu.VMEM((ng + 8, LANES), F32)],
        compiler_params=pltpu.CompilerParams(dimension_semantics=("parallel",), vmem_limit_bytes=VMEM_LIMIT),
        name="compress",
    )(kv, kv, wplane)


def _tap_matrix_t(ns, nc_pad):
    ratio = SEL_BLOCK // CMP_STRIDE
    taps = np.convolve(np.ones(ratio), np.ones(CMP_BLOCK // CMP_STRIDE))
    m = np.zeros((ns, nc_pad), np.float32)
    for j in range(ns):
        for i, tv in enumerate(taps):
            if ratio * j + i < nc_pad:
                m[j, ratio * j + i] = tv
    return jnp.asarray(m, dtype=BF16)


def _select_t(pslc_t, cur_t, ns):
    shape = pslc_t.shape
    blk = lax.broadcasted_iota(jnp.int32, shape, 0)
    forced = (blk == 0) | (blk == cur_t) | (blk == cur_t - 1)
    score = jnp.where(blk > cur_t, -jnp.inf, jnp.where(forced, jnp.inf, pslc_t))
    rank = jnp.zeros(shape, jnp.int32)
    for i in range(ns):
        si = score[i:i + 1, :]
        beats = (si > score) | ((si == score) & (blk > i))
        rank = rank + beats.astype(jnp.int32)
    return ((rank < SEL_TOPK) & (blk < ns)).astype(F32)


def _nsa_prompt_kernel(q_ref, kc_ref, sk_ref, sv_ref, w_ref, br_ref, g_ref, tap_ref, o_ref,
                       comb_ref, m_ref, l_ref, acc_ref, *, tq, tk, nc, ns):
    i = pl.program_id(1)
    start = i * tq
    nc_pad = kc_ref.shape[1]
    qpos = start + lax.broadcasted_iota(jnp.int32, (tq, 1), 0)
    qpos_t = start + lax.broadcasted_iota(jnp.int32, (1, tq), 1)
    br = _sigmoid(br_ref[0])

    def gate(h, branch):
        c = N_NSA_BRANCH * h + branch
        return br[:, c:c + 1]

    kc = kc_ref[0, :, 0:LANES]
    vc = kc_ref[0, :, LANES:2 * LANES]
    n_idx = lax.broadcasted_iota(jnp.int32, (1, nc_pad), 1)
    c_end = n_idx * CMP_STRIDE + (CMP_BLOCK - 1)
    cmask = (c_end <= qpos) & (n_idx < nc)
    cdist = (qpos - c_end).astype(F32)
    wlen = WINDOW + tq
    ws = pl.multiple_of(start, tq)
    kw = w_ref[0, pl.ds(ws, wlen), 0:LANES]
    vw = w_ref[0, pl.ds(ws, wlen), LANES:2 * LANES]
    wd = (lax.broadcasted_iota(jnp.int32, (tq, wlen), 0) + WINDOW
          - lax.broadcasted_iota(jnp.int32, (tq, wlen), 1))
    wpos = start - WINDOW + lax.broadcasted_iota(jnp.int32, (1, wlen), 1)
    wmask = (wd >= 0) & (wd <= WINDOW) & (wpos >= 0)
    wdf = wd.astype(F32)

    kcol = lax.broadcasted_iota(jnp.int32, (tq, tk), 1)
    ej = lax.broadcasted_iota(jnp.int32, (ns, tk), 0)
    ec = lax.broadcasted_iota(jnp.int32, (ns, tk), 1) // SEL_BLOCK
    nkb = (start + tq + tk - 1) // tk

    for g in range(NSA_KV_HEADS):
        imp = jnp.zeros((tq, nc_pad), F32)
        for r in range(NSA_GROUP):
            h = g * NSA_GROUP + r
            q = q_ref[0, :, h * LANES:(h + 1) * LANES]
            p = _masked_softmax(_dot_nt(q, kc) - _slope(h) * cdist, cmask)
            imp = imp + p
            o_cmp = _dot(p.astype(BF16), vc)
            pw = _masked_softmax(_dot_nt(q, kw) - _slope(h) * wdf, wmask)
            o_win = _dot(pw.astype(BF16), vw)
            comb_ref[h] = gate(h, 0) * o_cmp + gate(h, 2) * o_win
            m_ref[h] = jnp.full((tq, LANES), NEG, F32)
            l_ref[h] = jnp.zeros((tq, LANES), F32)
            acc_ref[h] = jnp.zeros((tq, LANES), F32)
        ih, im, il = _split3(imp)
        tap = tap_ref[...]
        pslc_t = _dot_nt(tap, ih) + _dot_nt(tap, im) + _dot_nt(tap, il)
        sel = _select_t(pslc_t, qpos_t // SEL_BLOCK, ns).T.astype(BF16)

        def body(kb, c, g=g, sel=sel):
            ks = pl.multiple_of(kb * tk, tk)
            k = sk_ref[0, pl.ds(ks, tk), :]
            v = sv_ref[0, pl.ds(ks, tk), :]
            expand = (ej == kb * (tk // SEL_BLOCK) + ec).astype(BF16)
            dist = qpos - (ks + kcol)
            valid = (_dot(sel, expand) > 0.5) & (dist >= 0)
            distf = dist.astype(F32)
            for r in range(NSA_GROUP):
                h = g * NSA_GROUP + r
                q = q_ref[0, :, h * LANES:(h + 1) * LANES]
                s = jnp.where(valid, _dot_nt(q, k) - _slope(h) * distf, NEG)
                m_old = m_ref[h]
                m_new = jnp.maximum(m_old, jnp.max(s, axis=1, keepdims=True))
                e = jnp.where(valid, jnp.exp(s - m_new[:, 0:1]), 0.0)
                alpha = jnp.exp(m_old - m_new)
                l_ref[h] = alpha * l_ref[h] + jnp.sum(e, axis=1, keepdims=True)
                acc_ref[h] = alpha * acc_ref[h] + _dot(e.astype(BF16), v)
                m_ref[h] = m_new
            return c

        lax.fori_loop(0, nkb, body, 0)

    lane = lax.broadcasted_iota(jnp.int32, (tq, LANES), 1)
    for pair in range(NSA_HEADS // 2):
        outs = []
        for h in (2 * pair, 2 * pair + 1):
            o_slc = acc_ref[h] / jnp.maximum(l_ref[h], 1e-30)
            outs.append(comb_ref[h] + gate(h, 1) * o_slc)
        g = (2 * pair) // NSA_GROUP
        lo, hi = outs
        if g == 0:
            hi = pltpu.roll(hi, HEAD_DIM, axis=1)
        else:
            lo = pltpu.roll(lo, HEAD_DIM, axis=1)
        gt = g_ref[0, :, pair * LANES:(pair + 1) * LANES]
        o_ref[0, :, pair * LANES:(pair + 1) * LANES] = (
            jnp.where(lane < HEAD_DIM, lo, hi) * (gt * _sigmoid(gt))).astype(o_ref.dtype)


def _nsa_prompt(qp, kvb, kcb, winb_pad, gates, tq, tk):
    b, t, _ = qp.shape
    tq = min(tq, t)
    tk = min(tk, t)
    nc_pad = kcb.shape[1]
    nc = t // CMP_STRIDE - CMP_BLOCK // CMP_STRIDE + 1
    ns = t // SEL_BLOCK
    slc_k_blk = (2 * SB_WIDTH + 2 * NSA_KV_WIDTH) // LANES
    return pl.pallas_call(
        functools.partial(_nsa_prompt_kernel, tq=tq, tk=tk, nc=nc, ns=ns),
        grid=(b, t // tq),
        in_specs=[pl.BlockSpec((1, tq, Q_PLACED), lambda bi, i: (bi, i, 1)),
                  pl.BlockSpec((1, nc_pad, 2 * LANES), lambda bi, i: (bi, 0, 0)),
                  pl.BlockSpec((1, t, LANES), lambda bi, i: (bi, 0, slc_k_blk)),
                  pl.BlockSpec((1, t, LANES), lambda bi, i: (bi, 0, slc_k_blk + 1)),
                  pl.BlockSpec((1, t + WINDOW, 2 * LANES), lambda bi, i: (bi, 0, 0)),
                  pl.BlockSpec((1, tq, LANES), lambda bi, i: (bi, i, GATE_BR_BLOCK)),
                  pl.BlockSpec((1, tq, NSA_WIDTH), lambda bi, i: (bi, i, 1)),
                  pl.BlockSpec((ns, nc_pad), lambda bi, i: (0, 0))],
        out_specs=pl.BlockSpec((1, tq, NSA_WIDTH), lambda bi, i: (bi, i, 0)),
        out_shape=jax.ShapeDtypeStruct((b, t, NSA_WIDTH), BF16),
        scratch_shapes=[pltpu.VMEM((NSA_HEADS, tq, LANES), F32) for _ in range(4)],
        compiler_params=pltpu.CompilerParams(dimension_semantics=("parallel", "arbitrary"),
                                             vmem_limit_bytes=VMEM_LIMIT),
        name="nsa_prompt",
    )(qp, kcb, kvb, kvb, winb_pad, gates, gates, _tap_matrix_t(ns, nc_pad))


def _merge_kernel(osb_ref, onsa_ref, gsb_ref, gnsa_ref, x_ref, wsb_ref, wnsa_ref, wout_ref, bout_ref,
                  lng_ref, lnb_ref, y_ref, yb_ref, *, alpha):
    u_sb = _dot(osb_ref[...].astype(BF16), wsb_ref[...])
    u_nsa = _dot(onsa_ref[...].astype(BF16), wnsa_ref[...])
    mid = _sigmoid(gsb_ref[...]) * u_sb + _sigmoid(gnsa_ref[...]) * u_nsa
    h = _dot(mid.astype(BF16), wout_ref[...]) + bout_ref[...]
    xf = alpha * x_ref[...] + h
    mu = jnp.mean(xf, axis=-1, keepdims=True)
    xc = xf - mu
    var = jnp.mean(xc * xc, axis=-1, keepdims=True)
    y = xc * lax.rsqrt(var + LN_EPS) * lng_ref[...] + lnb_ref[...]
    y_ref[...] = y
    yb_ref[...] = y.astype(BF16)


def _merge(osb, onsa, gates, x, w_sb, w_nsa, w_out, b_out, ln_g, ln_b, alpha, tm):
    n = x.shape[0]
    tm = min(tm, n)
    row = lambda c: pl.BlockSpec((tm, c), lambda i: (i, 0))
    full = lambda r, c: pl.BlockSpec((r, c), lambda i: (0, 0))
    merge_blk = (SB_WIDTH + NSA_WIDTH) // D_MODEL
    return pl.pallas_call(
        functools.partial(_merge_kernel, alpha=alpha),
        grid=(n // tm,),
        in_specs=[row(SB_WIDTH), row(NSA_WIDTH),
                  pl.BlockSpec((tm, D_MODEL), lambda i: (i, merge_blk)),
                  pl.BlockSpec((tm, D_MODEL), lambda i: (i, merge_blk + 1)),
                  row(D_MODEL), full(SB_WIDTH, D_MODEL), full(NSA_WIDTH, D_MODEL), full(D_MODEL, D_MODEL),
                  full(1, D_MODEL), full(1, D_MODEL), full(1, D_MODEL)],
        out_specs=[row(D_MODEL), row(D_MODEL)],
        out_shape=[jax.ShapeDtypeStruct((n, D_MODEL), F32), jax.ShapeDtypeStruct((n, D_MODEL), BF16)],
        compiler_params=pltpu.CompilerParams(dimension_semantics=("parallel",), vmem_limit_bytes=VMEM_LIMIT),
        name="merge",
    )(osb, onsa, gates, gates, x, w_sb, w_nsa, w_out, b_out, ln_g, ln_b)


def _place_heads(w, tile_of_head):
    lead = w.shape[:-1]
    onehot = np.zeros((SB_HEADS, 2), np.float32)
    onehot[np.arange(SB_HEADS), tile_of_head] = 1.0
    w = w.reshape(lead + (SB_HEADS, 1, HEAD_DIM)) * jnp.asarray(onehot)[:, :, None]
    return w.reshape(lead + (SB_HEADS * LANES,))


def _prepare_weights(w_in, b_in):
    def sections(a):
        sb_q = _place_heads(a[..., OFF_SB_Q:OFF_SB_Q + SB_WIDTH] * SCALE, np.arange(SB_HEADS) % 2)
        nsa_q = _place_heads(a[..., OFF_NSA_Q:OFF_NSA_Q + NSA_WIDTH] * SCALE, np.arange(NSA_HEADS) // NSA_GROUP)
        br = a[..., OFF_BR_GATE:OFF_MERGE]
        br = jnp.pad(br, [(0, 0)] * (a.ndim - 1) + [(0, LANES - br.shape[-1])])
        gates = jnp.concatenate([a[..., OFF_SB_GATE:OFF_BR_GATE], a[..., OFF_MERGE:], br], axis=-1)
        return a[..., :KV_CH], a[..., OFF_WIN:OFF_WIN + WIN_CH], jnp.concatenate([sb_q, nsa_q], axis=-1), gates
    ws = [s.astype(BF16) for s in sections(w_in)]
    bs = [s[:, None, :] for s in sections(b_in)]
    return ws, bs


def _project(xb, ws, bs, l, tm):
    kv, kvb = _linear(xb, ws[0][l], bs[0][l], (F32, BF16), tm, 512)
    win, winb = _linear(xb, ws[1][l], bs[1][l], (F32, BF16), tm, WIN_CH)
    (qp,) = _linear(xb, ws[2][l], bs[2][l], (BF16,), tm, 512)
    (gates,) = _linear(xb, ws[3][l], bs[3][l], (F32,), tm, 640)
    return kv, kvb, win, winb, qp, gates


def _prompt_layer(x, xb, ws, bs, l, wplane, w_sb, w_nsa, w_out, b_out, ln_g, ln_b, alpha, b, t):
    kv, kvb, win, winb, qp, gates = _project(xb, ws, bs, l, 1024)
    to3 = lambda a: a.reshape(b, t, a.shape[-1])
    kv3, kvb3, qp3, gates3 = to3(kv), to3(kvb), to3(qp), to3(gates)
    osb = _sb_prompt(qp3, kvb3, gates3, 256)
    kcb = _compress(kv3, wplane)
    winb_pad = jnp.pad(to3(winb), ((0, 0), (WINDOW, 0), (0, 0)))
    onsa = _nsa_prompt(qp3, kvb3, kcb, winb_pad, gates3, 128, 256)
    y, yb = _merge(osb.reshape(b * t, SB_WIDTH), onsa.reshape(b * t, NSA_WIDTH), gates, x,
                   w_sb, w_nsa, w_out, b_out, ln_g, ln_b, alpha, 512)
    return y, yb, kv3, to3(win)[:, -min(WINDOW, t):]


def _sample_kernel(pt_ref, cache_ref, qsb_ref, qn_ref, kvn_ref, winn_ref, winb_ref, gsb_ref, gnsa_ref, br_ref,
                   u_ref, tap_ref, wcmp_ref, osb_ref, onsa_ref, wout_ref,
                   buf, newbuf, cmpbuf, sh_ref, acc_ref, car_ref, sem, *, layer, n_pages, tn, kb):
    b = pl.program_id(0)
    nb = pl.num_programs(0)
    slot = b % 2
    past = n_pages * PAGE_SIZE
    rows = SB_HEADS * tn
    n_buf = winb_ref.shape[1]

    def page_copy(seq, p, slot_):
        return pltpu.make_async_copy(cache_ref.at[layer, pt_ref[seq, p]],
                                     buf.at[slot_, pl.ds(p * PAGE_SIZE, PAGE_SIZE), :], sem.at[slot_])

    @pl.when(b == 0)
    def _():
        for p in range(n_pages):
            page_copy(0, p, 0).start()

    @pl.when(b + 1 < nb)
    def _():
        for p in range(n_pages):
            page_copy(b + 1, p, 1 - slot).start()

    for p in range(n_pages):
        page_copy(b, p, slot).wait()

    row = lax.broadcasted_iota(jnp.int32, (rows, 1), 0)
    tok = row % tn
    qpos = past + tok
    newbuf[...] = jnp.zeros(newbuf.shape, F32)
    newbuf[0:tn, :] = kvn_ref[0]
    ncol = lax.broadcasted_iota(jnp.int32, (1, LANES), 1)

    qbd = qsb_ref[0]
    u = u_ref[...]
    mask_new = ncol < tok
    z = _dot_nt(qbd, newbuf[:, 0:SB_WIDTH].astype(BF16))
    ls = jnp.where(mask_new, -_softplus(z), 0.0)
    hi, lo = _split2(ls)
    un = u[0:LANES, 0:LANES]
    a = jnp.where(mask_new, jnp.exp(z + ls + _dot(hi, un) + _dot(lo, un)), 0.0)
    acc_ref[...] = _dot(a.astype(BF16), newbuf[:, SB_WIDTH:2 * SB_WIDTH].astype(BF16))
    car_ref[...] = jnp.broadcast_to(jnp.sum(ls, axis=1, keepdims=True), (rows, LANES))
    reps = kb // LANES

    def sb_body(s, c):
        ks = pl.multiple_of(past - (s + 1) * kb, kb)
        k = buf[slot, pl.ds(ks, kb), 0:SB_WIDTH].astype(BF16)
        v = buf[slot, pl.ds(ks, kb), SB_WIDTH:2 * SB_WIDTH].astype(BF16)
        c1 = car_ref[...]
        z = _dot_nt(qbd, k)
        ls = -_softplus(z)
        hi, lo = _split2(ls)
        a = jnp.exp(z + ls + _dot(hi, u) + _dot(lo, u) + jnp.concatenate([c1] * reps, axis=1))
        acc_ref[...] += _dot(a.astype(BF16), v)
        car_ref[...] = c1 + jnp.sum(ls, axis=1, keepdims=True)
        return c

    lax.fori_loop(0, past // kb, sb_body, 0, unroll=True)
    hrow = lax.broadcasted_iota(jnp.int32, (rows, SB_WIDTH), 0) // tn
    hcol = lax.broadcasted_iota(jnp.int32, (rows, SB_WIDTH), 1) // HEAD_DIM
    own_head = hrow == hcol

    def fold(x):
        return jnp.sum(jnp.where(own_head, x, 0.0).reshape(SB_HEADS, tn, SB_WIDTH), axis=0)

    gs = gsb_ref[0]
    osb_ref[0] = fold(acc_ref[...]) * (gs * _sigmoid(gs))

    qn = qn_ref[0]
    slope = jnp.exp2(-((row // tn) + 1).astype(F32) * (8.0 / NSA_HEADS))
    brs = _sigmoid(br_ref[0])

    ng = past // CMP_STRIDE
    off = 2 * SB_WIDTH
    kvc = []
    for c in range(2):
        cmpbuf[...] = buf[slot, :, off + c * LANES:off + (c + 1) * LANES]
        a_ = jnp.zeros((ng, LANES), F32)
        bh = jnp.zeros((ng, LANES), F32)
        for p in range(CMP_STRIDE):
            xp = cmpbuf[pl.ds(p, ng, stride=CMP_STRIDE), :]
            a_ = a_ + xp * wcmp_ref[c, p:p + 1, :]
            bh = bh + xp * wcmp_ref[c, CMP_STRIDE + p:CMP_STRIDE + p + 1, :]
        sh_ref[0:ng, :] = bh
        sh_ref[ng:ng + 8, :] = jnp.zeros((8, LANES), F32)
        kvc.append((a_ + sh_ref[1:ng + 1, :]).astype(BF16))
    kc, vc = kvc
    n_idx = lax.broadcasted_iota(jnp.int32, (1, ng), 1)
    c_end = n_idx * CMP_STRIDE + (CMP_BLOCK - 1)
    cmask = (c_end <= qpos) & (c_end < past)
    p_cmp = _masked_softmax(_dot_nt(qn, kc) - slope * (qpos - c_end).astype(F32), cmask)
    o_cmp = _dot(p_cmp.astype(BF16), vc)

    ns = tap_ref.shape[0]
    imp = jnp.sum(p_cmp.reshape(NSA_KV_HEADS, NSA_GROUP, tn, ng), axis=1).reshape(NSA_KV_HEADS * tn, ng)
    imp = jnp.concatenate([imp, jnp.zeros((LANES - NSA_KV_HEADS * tn, ng), F32)], axis=0)
    ih, im, il = _split3(imp)
    tap = tap_ref[...]
    pslc_t = _dot_nt(tap, ih) + _dot_nt(tap, im) + _dot_nt(tap, il)
    cur_t = (past + lax.broadcasted_iota(jnp.int32, (1, LANES), 1) % tn) // SEL_BLOCK
    n_sel = (past + tn + SEL_BLOCK - 1) // SEL_BLOCK
    sel = _select_t(pslc_t, cur_t, n_sel).T
    sel_rows = jnp.concatenate([sel[g * tn:(g + 1) * tn] for g in range(NSA_KV_HEADS) for _ in range(NSA_GROUP)],
                               axis=0).astype(BF16)
    ej = lax.broadcasted_iota(jnp.int32, (ns, past), 0)
    ec = lax.broadcasted_iota(jnp.int32, (ns, past), 1) // SEL_BLOCK
    valid_p = _dot(sel_rows, (ej == ec).astype(BF16)) > 0.5

    off = 2 * SB_WIDTH + 2 * NSA_KV_WIDTH
    pcol = lax.broadcasted_iota(jnp.int32, (1, past), 1)
    s_p = jnp.where(valid_p, _dot_nt(qn, buf[slot, :, off:off + LANES].astype(BF16))
                    - slope * (qpos - pcol).astype(F32), NEG)
    valid_n = ncol <= tok
    s_n = jnp.where(valid_n, _dot_nt(qn, newbuf[:, off:off + LANES].astype(BF16))
                    - slope * (tok - ncol).astype(F32), NEG)
    m = jnp.maximum(jnp.max(s_p, axis=1, keepdims=True), jnp.max(s_n, axis=1, keepdims=True))
    e_p = jnp.where(valid_p, jnp.exp(s_p - m), 0.0)
    e_n = jnp.where(valid_n, jnp.exp(s_n - m), 0.0)
    inv = 1.0 / jnp.maximum(jnp.sum(e_p, axis=1, keepdims=True) + jnp.sum(e_n, axis=1, keepdims=True), 1e-30)
    o_slc = (_dot((e_p * inv).astype(BF16), buf[slot, :, off + LANES:off + 2 * LANES].astype(BF16))
             + _dot((e_n * inv).astype(BF16), newbuf[:, off + LANES:off + 2 * LANES].astype(BF16)))

    winn = jnp.concatenate([winn_ref[0], jnp.zeros((LANES - tn, WIN_CH), F32)], axis=0)
    wcol = lax.broadcasted_iota(jnp.int32, (1, n_buf), 1)
    wd = tok + n_buf - wcol
    valid_w = (wd >= 0) & (wd <= WINDOW) & (past - n_buf + wcol >= 0)
    s_w = jnp.where(valid_w, _dot_nt(qn, winb_ref[0, :, 0:LANES].astype(BF16)) - slope * wd.astype(F32), NEG)
    s_n = jnp.where(valid_n, _dot_nt(qn, winn[:, 0:LANES].astype(BF16)) - slope * (tok - ncol).astype(F32), NEG)
    m = jnp.maximum(jnp.max(s_w, axis=1, keepdims=True), jnp.max(s_n, axis=1, keepdims=True))
    e_w = jnp.where(valid_w, jnp.exp(s_w - m), 0.0)
    e_n = jnp.where(valid_n, jnp.exp(s_n - m), 0.0)
    inv = 1.0 / jnp.maximum(jnp.sum(e_w, axis=1, keepdims=True) + jnp.sum(e_n, axis=1, keepdims=True), 1e-30)
    o_win = (_dot((e_w * inv).astype(BF16), winb_ref[0, :, LANES:2 * LANES].astype(BF16))
             + _dot((e_n * inv).astype(BF16), winn[:, LANES:2 * LANES].astype(BF16)))

    o_rows = brs[:, 0:1] * o_cmp + brs[:, 1:2] * o_slc + brs[:, 2:3] * o_win
    lane = lax.broadcasted_iota(jnp.int32, (rows, LANES), 1)
    rolled = pltpu.roll(o_rows, HEAD_DIM, axis=1)
    g0 = jnp.where(lane < HEAD_DIM, o_rows, rolled)
    g1 = jnp.where(lane < HEAD_DIM, rolled, o_rows)
    tiles = [g0] * (NSA_GROUP // 2) + [g1] * (NSA_GROUP // 2)
    gn = gnsa_ref[0]
    onsa_ref[0] = fold(jnp.concatenate(tiles, axis=1)) * (gn * _sigmoid(gn))

    wout_ref[0, 0:n_buf - tn, :] = winb_ref[0, tn:n_buf, :]
    wout_ref[0, n_buf - tn:n_buf, :] = winn_ref[0]


def _sample_attention(page_table, cache_kv, layer, qp, kv, win, state_win_l, gates, wplane, tn):
    bd, n_pages = page_table.shape
    past = n_pages * PAGE_SIZE
    n_buf = state_win_l.shape[1]
    rows = SB_HEADS * tn
    kb = 256
    assert NSA_KV_HEADS == 2 and SEL_TOPK >= 3 and past % kb == 0 and tn == 8 and n_buf >= tn
    assert past % SEL_BLOCK == 0 and tn <= SEL_BLOCK
    ng = past // CMP_STRIDE
    ns = -(-(past + tn) // SEL_BLOCK)
    ns_pad = -(-ns // SEL_BLOCK) * SEL_BLOCK
    assert past % CMP_STRIDE == 0 and tn < CMP_STRIDE
    q4 = qp.reshape(bd, tn, 2, SB_HEADS, LANES).transpose(0, 2, 3, 1, 4)
    col_tile = np.zeros((SB_HEADS, SB_WIDTH // LANES), np.float32)
    col_tile[np.arange(SB_HEADS), np.arange(SB_HEADS) // 2] = 1.0
    qsb = (q4[:, 0, :, :, None, :] * jnp.asarray(col_tile, BF16)[None, :, None, :, None]).reshape(bd, rows, SB_WIDTH)
    qn = q4[:, 1].reshape(bd, rows, LANES)
    gates3 = gates.reshape(bd, tn, GATE_COLS)
    br = gates3[:, :, GATE_BR_BLOCK * LANES:GATE_BR_BLOCK * LANES + N_NSA_BRANCH * NSA_HEADS]
    br = br.reshape(bd, tn, NSA_HEADS, N_NSA_BRANCH).transpose(0, 2, 1, 3).reshape(bd, rows, N_NSA_BRANCH)
    br = jnp.pad(br, ((0, 0), (0, 0), (0, LANES - N_NSA_BRANCH)))
    seq = lambda r, c, blk=0: pl.BlockSpec((1, r, c), lambda i, pt: (i, 0, blk))
    const2 = lambda r, c: pl.BlockSpec((r, c), lambda i, pt: (0, 0))
    grid_spec = pltpu.PrefetchScalarGridSpec(
        num_scalar_prefetch=1,
        grid=(bd,),
        in_specs=[pl.BlockSpec(memory_space=pl.ANY),
                  seq(rows, SB_WIDTH), seq(rows, LANES), seq(tn, KV_CH), seq(tn, WIN_CH), seq(n_buf, WIN_CH),
                  seq(tn, SB_WIDTH, 0), seq(tn, NSA_WIDTH, 1), seq(rows, LANES),
                  const2(kb, kb), const2(ns_pad, ng),
                  pl.BlockSpec((2, CMP_BLOCK, LANES), lambda i, pt: (0, 0, 0))],
        out_specs=[seq(tn, SB_WIDTH), seq(tn, NSA_WIDTH), seq(n_buf, WIN_CH)],
        scratch_shapes=[pltpu.VMEM((2, past, KV_CH), F32), pltpu.VMEM((LANES, KV_CH), F32),
                        pltpu.VMEM((past, LANES), F32), pltpu.VMEM((ng + 8, LANES), F32),
                        pltpu.VMEM((rows, SB_WIDTH), F32), pltpu.VMEM((rows, LANES), F32),
                        pltpu.SemaphoreType.DMA((2,))])
    tap = jnp.pad(_tap_matrix_t(ns, ng), ((0, ns_pad - ns), (0, 0)))
    return pl.pallas_call(
        functools.partial(_sample_kernel, layer=layer, n_pages=n_pages, tn=tn, kb=kb),
        grid_spec=grid_spec,
        out_shape=[jax.ShapeDtypeStruct((bd, tn, SB_WIDTH), F32), jax.ShapeDtypeStruct((bd, tn, NSA_WIDTH), F32),
                   jax.ShapeDtypeStruct((bd, n_buf, WIN_CH), F32)],
        compiler_params=pltpu.CompilerParams(dimension_semantics=("arbitrary",), vmem_limit_bytes=VMEM_LIMIT),
        name="sample_attention",
    )(page_table, cache_kv, qsb, qn, kv.reshape(bd, tn, KV_CH), win.reshape(bd, tn, WIN_CH), state_win_l,
      gates3, gates3, br, _upper_ones(kb), tap, wplane)


def _sample_layer(x, xb, ws, bs, l, page_table, cache_kv, state_win_l, wplane, w_sb, w_nsa, w_out, b_out,
                  ln_g, ln_b, alpha, bd, tn):
    kv, _, win, _, qp, gates = _project(xb, ws, bs, l, 1024)
    osb, onsa, win_out = _sample_attention(page_table, cache_kv, l, qp, kv, win, state_win_l, gates, wplane, tn)
    y, yb = _merge(osb.reshape(bd * tn, SB_WIDTH), onsa.reshape(bd * tn, NSA_WIDTH), gates, x,
                   w_sb, w_nsa, w_out, b_out, ln_g, ln_b, alpha, 512)
    return y, yb, kv.reshape(bd, tn, KV_CH), win_out


def kernel(x_prompt, x_sample, cache_kv, state_win, page_table, w_in, b_in, cmp_w, w_sb, w_nsa, w_out, b_out, ln_g, ln_b):
    depth = w_in.shape[0]
    alpha = float((2 * depth) ** 0.25)
    b, t, _ = x_prompt.shape
    bd, tn, _ = x_sample.shape
    ws, bs = _prepare_weights(w_in, b_in)
    xp = x_prompt.reshape(b * t, D_MODEL)
    xs = x_sample.reshape(bd * tn, D_MODEL)
    xpb, xsb = xp.astype(BF16), xs.astype(BF16)
    kv_p, kv_s, win_p, win_s = [], [], [], []
    for l in range(depth):
        shared = (_cmp_plane(cmp_w[l]), w_sb[l].astype(BF16), w_nsa[l].astype(BF16), w_out[l].astype(BF16),
                  b_out[l][None], ln_g[l][None], ln_b[l][None], alpha)
        xp, xpb, kvp, wp = _prompt_layer(xp, xpb, ws, bs, l, *shared, b, t)
        xs, xsb, kvs, wsm = _sample_layer(xs, xsb, ws, bs, l, page_table, cache_kv, state_win[l], *shared, bd, tn)
        kv_p.append(kvp)
        kv_s.append(kvs)
        win_p.append(wp)
        win_s.append(wsm)
    return (xp.reshape(b, t, D_MODEL), xs.reshape(bd, tn, D_MODEL), jnp.stack(kv_p), jnp.stack(kv_s),
            jnp.stack(win_p), jnp.stack(win_s))
```

```python
import functools

import numpy as np
import jax
import jax.numpy as jnp
from jax import lax
from jax.experimental import pallas as pl
from jax.experimental.pallas import tpu as pltpu

F32 = jnp.float32
BF16 = jnp.bfloat16

D_MODEL = 1024
HEAD_DIM = 64
SB_HEADS = 8
SB_WIDTH = SB_HEADS * HEAD_DIM
NSA_HEADS = 8
NSA_KV_HEADS = 2
NSA_GROUP = NSA_HEADS // NSA_KV_HEADS
NSA_WIDTH = NSA_HEADS * HEAD_DIM
NSA_KV_WIDTH = NSA_KV_HEADS * HEAD_DIM
N_NSA_BRANCH = 3
CMP_BLOCK = 32
CMP_STRIDE = 16
SEL_BLOCK = 64
SEL_TOPK = 8
WINDOW = 512
PAGE_SIZE = 128
LN_EPS = 1e-5

KV_CH = 2 * SB_WIDTH + 4 * NSA_KV_WIDTH
WIN_CH = 2 * NSA_KV_WIDTH
OFF_WIN = KV_CH
OFF_SB_Q = OFF_WIN + WIN_CH
OFF_NSA_Q = OFF_SB_Q + SB_WIDTH
OFF_SB_GATE = OFF_NSA_Q + NSA_WIDTH
OFF_NSA_GATE = OFF_SB_GATE + SB_WIDTH
OFF_BR_GATE = OFF_NSA_GATE + NSA_WIDTH
OFF_MERGE = OFF_BR_GATE + N_NSA_BRANCH * NSA_HEADS
IN_COLS = OFF_MERGE + 2 * D_MODEL

LANES = 128
Q_PLACED = SB_HEADS * LANES
GATE_COLS = SB_WIDTH + NSA_WIDTH + 2 * D_MODEL + LANES
GATE_BR_BLOCK = (SB_WIDTH + NSA_WIDTH + 2 * D_MODEL) // LANES
SCALE = HEAD_DIM ** -0.5
NEG = -1e30
LOG2E = float(np.log2(np.e))
VMEM_LIMIT = 56 * 1024 * 1024

_NT = (((1,), (1,)), ((), ()))


def _slope(h):
    return float(2.0 ** (-8.0 * (h + 1) / NSA_HEADS))


def _dot(a, b):
    return jnp.dot(a, b, preferred_element_type=F32)


def _dot_nt(a, b):
    return lax.dot_general(a, b, _NT, preferred_element_type=F32)


def _split2(x):
    hi = x.astype(BF16)
    lo = (x - hi.astype(F32)).astype(BF16)
    return hi, lo


def _split3(x):
    hi = x.astype(BF16)
    r = x - hi.astype(F32)
    mid = r.astype(BF16)
    lo = (r - mid.astype(F32)).astype(BF16)
    return hi, mid, lo


def _sigmoid(x):
    return 1.0 / (1.0 + jnp.exp(-x))


def _masked_softmax(s, mask):
    s = jnp.where(mask, s, NEG)
    m = jnp.max(s, axis=-1, keepdims=True)
    e = jnp.where(mask, jnp.exp(s - m), 0.0)
    return e / jnp.maximum(jnp.sum(e, axis=-1, keepdims=True), 1e-30)


def _linear_kernel(x_ref, w_ref, b_ref, *out_refs):
    acc = _dot(x_ref[...], w_ref[...]) + b_ref[...]
    for o in out_refs:
        o[...] = acc.astype(o.dtype)


def _linear(x, w, b, out_dtypes, tm, tn):
    n, k = x.shape
    c = w.shape[1]
    tm = min(tm, n)
    assert n % tm == 0 and c % tn == 0
    return pl.pallas_call(
        _linear_kernel,
        grid=(n // tm, c // tn),
        in_specs=[pl.BlockSpec((tm, k), lambda i, j: (i, 0)),
                  pl.BlockSpec((k, tn), lambda i, j: (0, j)),
                  pl.BlockSpec((1, tn), lambda i, j: (0, j))],
        out_specs=[pl.BlockSpec((tm, tn), lambda i, j: (i, j)) for _ in out_dtypes],
        out_shape=[jax.ShapeDtypeStruct((n, c), dt) for dt in out_dtypes],
        compiler_params=pltpu.CompilerParams(dimension_semantics=("parallel", "parallel"),
                                             vmem_limit_bytes=VMEM_LIMIT),
        name="linear",
    )(x, w, b)


def _sb_tile(q, k, v, u2, carry, mask):
    n = _dot_nt(q, k) * (-LOG2E)
    nabs = pltpu.bitcast(pltpu.bitcast(n, jnp.uint32) | jnp.uint32(0x80000000), F32)
    ls = jnp.minimum(n, 0.0) - jnp.log2(1.0 + jnp.exp2(nabs))
    if mask is not None:
        ls = jnp.where(mask, ls, 0.0)
    hi, lo = _split2(ls)
    rest = _dot(jnp.concatenate([hi, lo], axis=1), u2)
    a = jnp.exp2((ls - n) + rest + carry)
    if mask is not None:
        a = jnp.where(mask, a, 0.0)
    return _dot(a.astype(BF16), v), rest[:, 0:1] + ls[:, 0:1]


def _sb_prompt_kernel(q_ref, k_ref, v_ref, g_ref, u_ref, o_ref, acc_ref, car_ref, *, tq, tk):
    i = pl.program_id(2)
    u2 = u_ref[...]
    n_diag = tq // tk
    j_last = (i + 1) * n_diag - 1
    reps = tk // LANES
    acc_ref[...] = jnp.zeros(acc_ref.shape, F32)
    car_ref[...] = jnp.zeros(car_ref.shape, F32)

    def tile_pair(j, r0, masked):
        st = pl.multiple_of(j * tk, tk)
        k = k_ref[0, pl.ds(st, tk), :]
        v = v_ref[0, pl.ds(st, tk), :]
        mask = None
        if masked:
            mask = (lax.broadcasted_iota(jnp.int32, (tq - r0, tk), 1)
                    < lax.broadcasted_iota(jnp.int32, (tq - r0, tk), 0))
        for hh in range(2):
            q = q_ref[0, r0:tq, hh * LANES:(hh + 1) * LANES]
            c1 = car_ref[hh, r0:tq, :]
            av, tot = _sb_tile(q, k, v, u2, jnp.concatenate([c1] * reps, axis=1), mask)
            acc_ref[hh, r0:tq, :] += av
            car_ref[hh, r0:tq, :] = c1 + tot

    for m in range(n_diag):
        tile_pair(j_last - m, (n_diag - 1 - m) * tk, True)

    def body(s, c):
        tile_pair(j_last - s, 0, False)
        return c

    lax.fori_loop(n_diag, j_last + 1, body, 0)
    lane = lax.broadcasted_iota(jnp.int32, (tq, LANES), 1)
    o = jnp.where(lane < HEAD_DIM, acc_ref[0], acc_ref[1])
    g = g_ref[0]
    o_ref[0] = (o * (g * _sigmoid(g))).astype(o_ref.dtype)


def _upper_ones2(t):
    r = np.arange(t)
    u = (r[:, None] > r[None, :]).astype(np.float32)
    return jnp.asarray(np.concatenate([u, u], axis=0), dtype=BF16)


def _sb_prompt(qp, kvb, gates, tq, tk):
    b, t, _ = qp.shape
    tq, tk = min(tq, t), min(tk, t)
    assert t % tq == 0 and tq % tk == 0
    hp = SB_HEADS // 2
    return pl.pallas_call(
        functools.partial(_sb_prompt_kernel, tq=tq, tk=tk),
        grid=(b, hp, t // tq),
        in_specs=[pl.BlockSpec((1, tq, 2 * LANES), lambda bi, h, i: (bi, i, h)),
                  pl.BlockSpec((1, t, LANES), lambda bi, h, i: (bi, 0, h)),
                  pl.BlockSpec((1, t, LANES), lambda bi, h, i: (bi, 0, hp + h)),
                  pl.BlockSpec((1, tq, LANES), lambda bi, h, i: (bi, i, h)),
                  pl.BlockSpec((2 * tk, tk), lambda bi, h, i: (0, 0))],
        out_specs=pl.BlockSpec((1, tq, LANES), lambda bi, h, i: (bi, i, h)),
        out_shape=jax.ShapeDtypeStruct((b, t, SB_WIDTH), BF16),
        scratch_shapes=[pltpu.VMEM((2, tq, LANES), F32), pltpu.VMEM((2, tq, LANES), F32)],
        compiler_params=pltpu.CompilerParams(dimension_semantics=("parallel", "parallel", "arbitrary"),
                                             vmem_limit_bytes=VMEM_LIMIT),
        name="sb_prompt",
    )(qp, kvb, kvb, gates, _upper_ones2(tk))


def _compress_kernel(k_ref, v_ref, w_ref, o_ref, sh_ref, *, ng):
    for c, x_ref in enumerate((k_ref, v_ref)):
        a = jnp.zeros((ng, LANES), F32)
        bh = jnp.zeros((ng, LANES), F32)
        for p in range(CMP_STRIDE):
            xp = x_ref[0, pl.ds(p, ng, stride=CMP_STRIDE), :]
            a = a + xp * w_ref[c, p:p + 1, :]
            bh = bh + xp * w_ref[c, CMP_STRIDE + p:CMP_STRIDE + p + 1, :]
        sh_ref[0:ng, :] = bh
        sh_ref[ng:ng + 8, :] = jnp.zeros((8, LANES), F32)
        o_ref[0, :, c * LANES:(c + 1) * LANES] = (a + sh_ref[1:ng + 1, :]).astype(o_ref.dtype)


def _cmp_plane(cmp_w_l):
    return jnp.broadcast_to(cmp_w_l[:, :, None], (2, CMP_BLOCK, LANES))


def _compress(kv, wplane):
    b, l, _ = kv.shape
    assert CMP_BLOCK == 2 * CMP_STRIDE
    ng = l // CMP_STRIDE
    cmp_k_blk = 2 * SB_WIDTH // LANES
    return pl.pallas_call(
        functools.partial(_compress_kernel, ng=ng),
        grid=(b,),
        in_specs=[pl.BlockSpec((1, l, LANES), lambda bi: (bi, 0, cmp_k_blk)),
                  pl.BlockSpec((1, l, LANES), lambda bi: (bi, 0, cmp_k_blk + 1)),
                  pl.BlockSpec((2, CMP_BLOCK, LANES), lambda bi: (0, 0, 0))],
        out_specs=pl.BlockSpec((1, ng, 2 * LANES), lambda bi: (bi, 0, 0)),
        out_shape=jax.ShapeDtypeStruct((b, ng, 2 * LANES), BF16),
        scratch_shapes=[pltpu.VMEM((ng + 8, LANES), F32)],
        compiler_params=pltpu.CompilerParams(dimension_semantics=("parallel",), vmem_limit_bytes=VMEM_LIMIT),
        name="compress",
    )(kv, kv, wplane)


def _tap_matrix_t(ns, nc_pad):
    ratio = SEL_BLOCK // CMP_STRIDE
    taps = np.convolve(np.ones(ratio), np.ones(CMP_BLOCK // CMP_STRIDE))
    m = np.zeros((ns, nc_pad), np.float32)
    for j in range(ns):
        for i, tv in enumerate(taps):
            if ratio * j + i < nc_pad:
                m[j, ratio * j + i] = tv
    return jnp.asarray(m, dtype=BF16)


def _select_t(pslc_t, cur_t, ns):
    shape = pslc_t.shape
    blk = lax.broadcasted_iota(jnp.int32, shape, 0)
    forced = (blk == 0) | (blk == cur_t) | (blk == cur_t - 1)
    score = jnp.where(blk > cur_t, -jnp.inf, jnp.where(forced, jnp.inf, pslc_t))
    rank = jnp.zeros(shape, jnp.int32)
    for i in range(ns):
        si = score[i:i + 1, :]
        beats = (si > score) | ((si == score) & (blk > i))
        rank = rank + beats.astype(jnp.int32)
    return ((rank < SEL_TOPK) & (blk < ns)).astype(F32)


def _nsa_prompt_kernel(q_ref, kc_ref, sk_ref, sv_ref, w_ref, br_ref, g_ref, tap_ref, o_ref,
                       comb_ref, m_ref, l_ref, acc_ref, *, tq, tk, nc, ns):
    i = pl.program_id(1)
    start = i * tq
    nc_pad = kc_ref.shape[1]
    qpos = start + lax.broadcasted_iota(jnp.int32, (tq, 1), 0)
    qpos_t = start + lax.broadcasted_iota(jnp.int32, (1, tq), 1)
    br = _sigmoid(br_ref[0])

    def gate(h, branch):
        c = N_NSA_BRANCH * h + branch
        return br[:, c:c + 1]

    kc = kc_ref[0, :, 0:LANES]
    vc = kc_ref[0, :, LANES:2 * LANES]
    n_idx = lax.broadcasted_iota(jnp.int32, (1, nc_pad), 1)
    c_end = n_idx * CMP_STRIDE + (CMP_BLOCK - 1)
    cmask = (c_end <= qpos) & (n_idx < nc)
    cdist = (qpos - c_end).astype(F32)
    wlen = WINDOW + tq
    ws = pl.multiple_of(start, tq)
    kw = w_ref[0, pl.ds(ws, wlen), 0:LANES]
    vw = w_ref[0, pl.ds(ws, wlen), LANES:2 * LANES]
    wd = (lax.broadcasted_iota(jnp.int32, (tq, wlen), 0) + WINDOW
          - lax.broadcasted_iota(jnp.int32, (tq, wlen), 1))
    wpos = start - WINDOW + lax.broadcasted_iota(jnp.int32, (1, wlen), 1)
    wmask = (wd >= 0) & (wd <= WINDOW) & (wpos >= 0)
    wdf = wd.astype(F32)

    kcol = lax.broadcasted_iota(jnp.int32, (tq, tk), 1)
    ej = lax.broadcasted_iota(jnp.int32, (ns, tk), 0)
    ec = lax.broadcasted_iota(jnp.int32, (ns, tk), 1) // SEL_BLOCK
    nkb = (start + tq + tk - 1) // tk

    for g in range(NSA_KV_HEADS):
        imp = jnp.zeros((tq, nc_pad), F32)
        for r in range(NSA_GROUP):
            h = g * NSA_GROUP + r
            q = q_ref[0, :, h * LANES:(h + 1) * LANES]
            p = _masked_softmax(_dot_nt(q, kc) - _slope(h) * cdist, cmask)
            imp = imp + p
            o_cmp = _dot(p.astype(BF16), vc)
            pw = _masked_softmax(_dot_nt(q, kw) - _slope(h) * wdf, wmask)
            o_win = _dot(pw.astype(BF16), vw)
            comb_ref[h] = gate(h, 0) * o_cmp + gate(h, 2) * o_win
            m_ref[h] = jnp.full((tq, LANES), NEG, F32)
            l_ref[h] = jnp.zeros((tq, LANES), F32)
            acc_ref[h] = jnp.zeros((tq, LANES), F32)
        ih, im, il = _split3(imp)
        tap = tap_ref[...]
        pslc_t = _dot_nt(tap, ih) + _dot_nt(tap, im) + _dot_nt(tap, il)
        sel = _select_t(pslc_t, qpos_t // SEL_BLOCK, ns).T.astype(BF16)

        def body(kb, c, g=g, sel=sel):
            ks = pl.multiple_of(kb * tk, tk)
            k = sk_ref[0, pl.ds(ks, tk), :]
            v = sv_ref[0, pl.ds(ks, tk), :]
            expand = (ej == kb * (tk // SEL_BLOCK) + ec).astype(BF16)
            dist = qpos - (ks + kcol)
            valid = (_dot(sel, expand) > 0.5) & (dist >= 0)
            distf = dist.astype(F32)
            for r in range(NSA_GROUP):
                h = g * NSA_GROUP + r
                q = q_ref[0, :, h * LANES:(h + 1) * LANES]
                s = jnp.where(valid, _dot_nt(q, k) - _slope(h) * distf, NEG)
                m_old = m_ref[h]
                m_new = jnp.maximum(m_old, jnp.max(s, axis=1, keepdims=True))
                e = jnp.where(valid, jnp.exp(s - m_new[:, 0:1]), 0.0)
                alpha = jnp.exp(m_old - m_new)
                l_ref[h] = alpha * l_ref[h] + jnp.sum(e, axis=1, keepdims=True)
                acc_ref[h] = alpha * acc_ref[h] + _dot(e.astype(BF16), v)
                m_ref[h] = m_new
            return c

        lax.fori_loop(0, nkb, body, 0)

    lane = lax.broadcasted_iota(jnp.int32, (tq, LANES), 1)
    for pair in range(NSA_HEADS // 2):
        outs = []
        for h in (2 * pair, 2 * pair + 1):
            o_slc = acc_ref[h] / jnp.maximum(l_ref[h], 1e-30)
            outs.append(comb_ref[h] + gate(h, 1) * o_slc)
        g = (2 * pair) // NSA_GROUP
        lo, hi = outs
        if g == 0:
            hi = pltpu.roll(hi, HEAD_DIM, axis=1)
        else:
            lo = pltpu.roll(lo, HEAD_DIM, axis=1)
        gt = g_ref[0, :, pair * LANES:(pair + 1) * LANES]
        o_ref[0, :, pair * LANES:(pair + 1) * LANES] = (
            jnp.where(lane < HEAD_DIM, lo, hi) * (gt * _sigmoid(gt))).astype(o_ref.dtype)


def _nsa_prompt(qp, kvb, kcb, winb_pad, gates, tq, tk):
    b, t, _ = qp.shape
    tq = min(tq, t)
    tk = min(tk, t)
    nc_pad = kcb.shape[1]
    nc = t // CMP_STRIDE - CMP_BLOCK // CMP_STRIDE + 1
    ns = t // SEL_BLOCK
    slc_k_blk = (2 * SB_WIDTH + 2 * NSA_KV_WIDTH) // LANES
    return pl.pallas_call(
        functools.partial(_nsa_prompt_kernel, tq=tq, tk=tk, nc=nc, ns=ns),
        grid=(b, t // tq),
        in_specs=[pl.BlockSpec((1, tq, Q_PLACED), lambda bi, i: (bi, i, 1)),
                  pl.BlockSpec((1, nc_pad, 2 * LANES), lambda bi, i: (bi, 0, 0)),
                  pl.BlockSpec((1, t, LANES), lambda bi, i: (bi, 0, slc_k_blk)),
                  pl.BlockSpec((1, t, LANES), lambda bi, i: (bi, 0, slc_k_blk + 1)),
                  pl.BlockSpec((1, t + WINDOW, 2 * LANES), lambda bi, i: (bi, 0, 0)),
                  pl.BlockSpec((1, tq, LANES), lambda bi, i: (bi, i, GATE_BR_BLOCK)),
                  pl.BlockSpec((1, tq, NSA_WIDTH), lambda bi, i: (bi, i, 1)),
                  pl.BlockSpec((ns, nc_pad), lambda bi, i: (0, 0))],
        out_specs=pl.BlockSpec((1, tq, NSA_WIDTH), lambda bi, i: (bi, i, 0)),
        out_shape=jax.ShapeDtypeStruct((b, t, NSA_WIDTH), BF16),
        scratch_shapes=[pltpu.VMEM((NSA_HEADS, tq, LANES), F32) for _ in range(4)],
        compiler_params=pltpu.CompilerParams(dimension_semantics=("parallel", "arbitrary"),
                                             vmem_limit_bytes=VMEM_LIMIT),
        name="nsa_prompt",
    )(qp, kcb, kvb, kvb, winb_pad, gates, gates, _tap_matrix_t(ns, nc_pad))


def _merge_kernel(osb_ref, onsa_ref, gsb_ref, gnsa_ref, x_ref, wsb_ref, wnsa_ref, wout_ref, bout_ref,
                  lng_ref, lnb_ref, y_ref, yb_ref, *, alpha):
    u_sb = _dot(osb_ref[...].astype(BF16), wsb_ref[...])
    u_nsa = _dot(onsa_ref[...].astype(BF16), wnsa_ref[...])
    mid = _sigmoid(gsb_ref[...]) * u_sb + _sigmoid(gnsa_ref[...]) * u_nsa
    h = _dot(mid.astype(BF16), wout_ref[...]) + bout_ref[...]
    xf = alpha * x_ref[...] + h
    mu = jnp.mean(xf, axis=-1, keepdims=True)
    xc = xf - mu
    var = jnp.mean(xc * xc, axis=-1, keepdims=True)
    y = xc * lax.rsqrt(var + LN_EPS) * lng_ref[...] + lnb_ref[...]
    y_ref[...] = y
    yb_ref[...] = y.astype(BF16)


def _merge(osb, onsa, gates, x, w_sb, w_nsa, w_out, b_out, ln_g, ln_b, alpha, tm):
    n = x.shape[0]
    tm = min(tm, n)
    row = lambda c: pl.BlockSpec((tm, c), lambda i: (i, 0))
    full = lambda r, c: pl.BlockSpec((r, c), lambda i: (0, 0))
    merge_blk = (SB_WIDTH + NSA_WIDTH) // D_MODEL
    return pl.pallas_call(
        functools.partial(_merge_kernel, alpha=alpha),
        grid=(n // tm,),
        in_specs=[row(SB_WIDTH), row(NSA_WIDTH),
                  pl.BlockSpec((tm, D_MODEL), lambda i: (i, merge_blk)),
                  pl.BlockSpec((tm, D_MODEL), lambda i: (i, merge_blk + 1)),
                  row(D_MODEL), full(SB_WIDTH, D_MODEL), full(NSA_WIDTH, D_MODEL), full(D_MODEL, D_MODEL),
                  full(1, D_MODEL), full(1, D_MODEL), full(1, D_MODEL)],
        out_specs=[row(D_MODEL), row(D_MODEL)],
        out_shape=[jax.ShapeDtypeStruct((n, D_MODEL), F32), jax.ShapeDtypeStruct((n, D_MODEL), BF16)],
        compiler_params=pltpu.CompilerParams(dimension_semantics=("parallel",), vmem_limit_bytes=VMEM_LIMIT),
        name="merge",
    )(osb, onsa, gates, gates, x, w_sb, w_nsa, w_out, b_out, ln_g, ln_b)


def _place_heads(w, tile_of_head):
    lead = w.shape[:-1]
    onehot = np.zeros((SB_HEADS, 2), np.float32)
    onehot[np.arange(SB_HEADS), tile_of_head] = 1.0
    w = w.reshape(lead + (SB_HEADS, 1, HEAD_DIM)) * jnp.asarray(onehot)[:, :, None]
    return w.reshape(lead + (SB_HEADS * LANES,))


def _prepare_weights(w_in, b_in):
    def sections(a):
        sb_q = _place_heads(a[..., OFF_SB_Q:OFF_SB_Q + SB_WIDTH] * SCALE, np.arange(SB_HEADS) % 2)
        nsa_q = _place_heads(a[..., OFF_NSA_Q:OFF_NSA_Q + NSA_WIDTH] * SCALE, np.arange(NSA_HEADS) // NSA_GROUP)
        br = a[..., OFF_BR_GATE:OFF_MERGE]
        br = jnp.pad(br, [(0, 0)] * (a.ndim - 1) + [(0, LANES - br.shape[-1])])
        gates = jnp.concatenate([a[..., OFF_SB_GATE:OFF_BR_GATE], a[..., OFF_MERGE:], br], axis=-1)
        return a[..., :KV_CH], a[..., OFF_WIN:OFF_WIN + WIN_CH], jnp.concatenate([sb_q, nsa_q], axis=-1), gates
    ws = [s.astype(BF16) for s in sections(w_in)]
    bs = [s[:, None, :] for s in sections(b_in)]
    return ws, bs


def _project(xb, ws, bs, l, tm):
    kv, kvb = _linear(xb, ws[0][l], bs[0][l], (F32, BF16), tm, 512)
    win, winb = _linear(xb, ws[1][l], bs[1][l], (F32, BF16), tm, WIN_CH)
    (qp,) = _linear(xb, ws[2][l], bs[2][l], (BF16,), tm, 512)
    (gates,) = _linear(xb, ws[3][l], bs[3][l], (F32,), tm, 640)
    return kv, kvb, win, winb, qp, gates


def _prompt_layer(x, xb, ws, bs, l, wplane, w_sb, w_nsa, w_out, b_out, ln_g, ln_b, alpha, b, t):
    kv, kvb, win, winb, qp, gates = _project(xb, ws, bs, l, 1024)
    to3 = lambda a: a.reshape(b, t, a.shape[-1])
    kv3, kvb3, qp3, gates3 = to3(kv), to3(kvb), to3(qp), to3(gates)
    osb = _sb_prompt(qp3, kvb3, gates3, 1024, 256)
    kcb = _compress(kv3, wplane)
    winb_pad = jnp.pad(to3(winb), ((0, 0), (WINDOW, 0), (0, 0)))
    onsa = _nsa_prompt(qp3, kvb3, kcb, winb_pad, gates3, 128, 256)
    y, yb = _merge(osb.reshape(b * t, SB_WIDTH), onsa.reshape(b * t, NSA_WIDTH), gates, x,
                   w_sb, w_nsa, w_out, b_out, ln_g, ln_b, alpha, 512)
    return y, yb, kv3, to3(win)[:, -min(WINDOW, t):]


def _sample_kernel(pt_ref, cache_ref, qsb_ref, qn_ref, kvn_ref, winn_ref, winb_ref, gsb_ref, gnsa_ref, br_ref,
                   u_ref, tap_ref, wcmp_ref, osb_ref, onsa_ref, wout_ref,
                   buf, newbuf, cmpbuf, sh_ref, acc_ref, car_ref, sem, *, layer, n_pages, tn, kb):
    b = pl.program_id(0)
    nb = pl.num_programs(0)
    slot = b % 2
    past = n_pages * PAGE_SIZE
    rows = SB_HEADS * tn
    n_buf = winb_ref.shape[1]

    def page_copy(seq, p, slot_):
        return pltpu.make_async_copy(cache_ref.at[layer, pt_ref[seq, p]],
                                     buf.at[slot_, pl.ds(p * PAGE_SIZE, PAGE_SIZE), :], sem.at[slot_])

    @pl.when(b == 0)
    def _():
        for p in range(n_pages):
            page_copy(0, p, 0).start()

    @pl.when(b + 1 < nb)
    def _():
        for p in range(n_pages):
            page_copy(b + 1, p, 1 - slot).start()

    for p in range(n_pages):
        page_copy(b, p, slot).wait()

    row = lax.broadcasted_iota(jnp.int32, (rows, 1), 0)
    tok = row % tn
    qpos = past + tok
    newbuf[...] = jnp.zeros(newbuf.shape, F32)
    newbuf[0:tn, :] = kvn_ref[0]
    ncol = lax.broadcasted_iota(jnp.int32, (1, LANES), 1)

    qbd = qsb_ref[0]
    u2 = u_ref[...]
    un = u2[0:LANES, 0:LANES]
    av, tot = _sb_tile(qbd, newbuf[:, 0:SB_WIDTH].astype(BF16), newbuf[:, SB_WIDTH:2 * SB_WIDTH].astype(BF16),
                       jnp.concatenate([un, un], axis=0), jnp.zeros((rows, LANES), F32),
                       ncol < tok)
    acc_ref[...] = av
    car_ref[...] = jnp.broadcast_to(tot, (rows, LANES))
    reps = kb // LANES

    def sb_body(s, c):
        ks = pl.multiple_of(past - (s + 1) * kb, kb)
        k = buf[slot, pl.ds(ks, kb), 0:SB_WIDTH].astype(BF16)
        v = buf[slot, pl.ds(ks, kb), SB_WIDTH:2 * SB_WIDTH].astype(BF16)
        c1 = car_ref[...]
        av, tot = _sb_tile(qbd, k, v, u2, jnp.concatenate([c1] * reps, axis=1), None)
        acc_ref[...] += av
        car_ref[...] = c1 + tot
        return c

    lax.fori_loop(0, past // kb, sb_body, 0, unroll=True)
    hrow = lax.broadcasted_iota(jnp.int32, (rows, SB_WIDTH), 0) // tn
    hcol = lax.broadcasted_iota(jnp.int32, (rows, SB_WIDTH), 1) // HEAD_DIM
    own_head = hrow == hcol

    def fold(x):
        return jnp.sum(jnp.where(own_head, x, 0.0).reshape(SB_HEADS, tn, SB_WIDTH), axis=0)

    gs = gsb_ref[0]
    osb_ref[0] = fold(acc_ref[...]) * (gs * _sigmoid(gs))

    qn = qn_ref[0]
    slope = jnp.exp2(-((row // tn) + 1).astype(F32) * (8.0 / NSA_HEADS))
    brs = _sigmoid(br_ref[0])

    ng = past // CMP_STRIDE
    off = 2 * SB_WIDTH
    kvc = []
    for c in range(2):
        cmpbuf[...] = buf[slot, :, off + c * LANES:off + (c + 1) * LANES]
        a_ = jnp.zeros((ng, LANES), F32)
        bh = jnp.zeros((ng, LANES), F32)
        for p in range(CMP_STRIDE):
            xp = cmpbuf[pl.ds(p, ng, stride=CMP_STRIDE), :]
            a_ = a_ + xp * wcmp_ref[c, p:p + 1, :]
            bh = bh + xp * wcmp_ref[c, CMP_STRIDE + p:CMP_STRIDE + p + 1, :]
        sh_ref[0:ng, :] = bh
        sh_ref[ng:ng + 8, :] = jnp.zeros((8, LANES), F32)
        kvc.append((a_ + sh_ref[1:ng + 1, :]).astype(BF16))
    kc, vc = kvc
    n_idx = lax.broadcasted_iota(jnp.int32, (1, ng), 1)
    c_end = n_idx * CMP_STRIDE + (CMP_BLOCK - 1)
    cmask = (c_end <= qpos) & (c_end < past)
    p_cmp = _masked_softmax(_dot_nt(qn, kc) - slope * (qpos - c_end).astype(F32), cmask)
    o_cmp = _dot(p_cmp.astype(BF16), vc)

    ns = tap_ref.shape[0]
    imp = jnp.sum(p_cmp.reshape(NSA_KV_HEADS, NSA_GROUP, tn, ng), axis=1).reshape(NSA_KV_HEADS * tn, ng)
    imp = jnp.concatenate([imp, jnp.zeros((LANES - NSA_KV_HEADS * tn, ng), F32)], axis=0)
    ih, im, il = _split3(imp)
    tap = tap_ref[...]
    pslc_t = _dot_nt(tap, ih) + _dot_nt(tap, im) + _dot_nt(tap, il)
    cur_t = (past + lax.broadcasted_iota(jnp.int32, (1, LANES), 1) % tn) // SEL_BLOCK
    n_sel = (past + tn + SEL_BLOCK - 1) // SEL_BLOCK
    sel = _select_t(pslc_t, cur_t, n_sel).T
    sel_rows = jnp.concatenate([sel[g * tn:(g + 1) * tn] for g in range(NSA_KV_HEADS) for _ in range(NSA_GROUP)],
                               axis=0).astype(BF16)
    ej = lax.broadcasted_iota(jnp.int32, (ns, past), 0)
    ec = lax.broadcasted_iota(jnp.int32, (ns, past), 1) // SEL_BLOCK
    valid_p = _dot(sel_rows, (ej == ec).astype(BF16)) > 0.5

    off = 2 * SB_WIDTH + 2 * NSA_KV_WIDTH
    pcol = lax.broadcasted_iota(jnp.int32, (1, past), 1)
    s_p = jnp.where(valid_p, _dot_nt(qn, buf[slot, :, off:off + LANES].astype(BF16))
                    - slope * (qpos - pcol).astype(F32), NEG)
    valid_n = ncol <= tok
    s_n = jnp.where(valid_n, _dot_nt(qn, newbuf[:, off:off + LANES].astype(BF16))
                    - slope * (tok - ncol).astype(F32), NEG)
    m = jnp.maximum(jnp.max(s_p, axis=1, keepdims=True), jnp.max(s_n, axis=1, keepdims=True))
    e_p = jnp.where(valid_p, jnp.exp(s_p - m), 0.0)
    e_n = jnp.where(valid_n, jnp.exp(s_n - m), 0.0)
    inv = 1.0 / jnp.maximum(jnp.sum(e_p, axis=1, keepdims=True) + jnp.sum(e_n, axis=1, keepdims=True), 1e-30)
    o_slc = (_dot((e_p * inv).astype(BF16), buf[slot, :, off + LANES:off + 2 * LANES].astype(BF16))
             + _dot((e_n * inv).astype(BF16), newbuf[:, off + LANES:off + 2 * LANES].astype(BF16)))

    winn = jnp.concatenate([winn_ref[0], jnp.zeros((LANES - tn, WIN_CH), F32)], axis=0)
    wcol = lax.broadcasted_iota(jnp.int32, (1, n_buf), 1)
    wd = tok + n_buf - wcol
    valid_w = (wd >= 0) & (wd <= WINDOW) & (past - n_buf + wcol >= 0)
    s_w = jnp.where(valid_w, _dot_nt(qn, winb_ref[0, :, 0:LANES].astype(BF16)) - slope * wd.astype(F32), NEG)
    s_n = jnp.where(valid_n, _dot_nt(qn, winn[:, 0:LANES].astype(BF16)) - slope * (tok - ncol).astype(F32), NEG)
    m = jnp.maximum(jnp.max(s_w, axis=1, keepdims=True), jnp.max(s_n, axis=1, keepdims=True))
    e_w = jnp.where(valid_w, jnp.exp(s_w - m), 0.0)
    e_n = jnp.where(valid_n, jnp.exp(s_n - m), 0.0)
    inv = 1.0 / jnp.maximum(jnp.sum(e_w, axis=1, keepdims=True) + jnp.sum(e_n, axis=1, keepdims=True), 1e-30)
    o_win = (_dot((e_w * inv).astype(BF16), winb_ref[0, :, LANES:2 * LANES].astype(BF16))
             + _dot((e_n * inv).astype(BF16), winn[:, LANES:2 * LANES].astype(BF16)))

    o_rows = brs[:, 0:1] * o_cmp + brs[:, 1:2] * o_slc + brs[:, 2:3] * o_win
    lane = lax.broadcasted_iota(jnp.int32, (rows, LANES), 1)
    rolled = pltpu.roll(o_rows, HEAD_DIM, axis=1)
    g0 = jnp.where(lane < HEAD_DIM, o_rows, rolled)
    g1 = jnp.where(lane < HEAD_DIM, rolled, o_rows)
    tiles = [g0] * (NSA_GROUP // 2) + [g1] * (NSA_GROUP // 2)
    gn = gnsa_ref[0]
    onsa_ref[0] = fold(jnp.concatenate(tiles, axis=1)) * (gn * _sigmoid(gn))

    wout_ref[0, 0:n_buf - tn, :] = winb_ref[0, tn:n_buf, :]
    wout_ref[0, n_buf - tn:n_buf, :] = winn_ref[0]


def _sample_attention(page_table, cache_kv, layer, qp, kv, win, state_win_l, gates, wplane, tn):
    bd, n_pages = page_table.shape
    past = n_pages * PAGE_SIZE
    n_buf = state_win_l.shape[1]
    rows = SB_HEADS * tn
    kb = 256
    assert NSA_KV_HEADS == 2 and SEL_TOPK >= 3 and past % kb == 0 and tn == 8 and n_buf >= tn
    assert past % SEL_BLOCK == 0 and tn <= SEL_BLOCK
    ng = past // CMP_STRIDE
    ns = -(-(past + tn) // SEL_BLOCK)
    ns_pad = -(-ns // SEL_BLOCK) * SEL_BLOCK
    assert past % CMP_STRIDE == 0 and tn < CMP_STRIDE
    q4 = qp.reshape(bd, tn, 2, SB_HEADS, LANES).transpose(0, 2, 3, 1, 4)
    col_tile = np.zeros((SB_HEADS, SB_WIDTH // LANES), np.float32)
    col_tile[np.arange(SB_HEADS), np.arange(SB_HEADS) // 2] = 1.0
    qsb = (q4[:, 0, :, :, None, :] * jnp.asarray(col_tile, BF16)[None, :, None, :, None]).reshape(bd, rows, SB_WIDTH)
    qn = q4[:, 1].reshape(bd, rows, LANES)
    gates3 = gates.reshape(bd, tn, GATE_COLS)
    br = gates3[:, :, GATE_BR_BLOCK * LANES:GATE_BR_BLOCK * LANES + N_NSA_BRANCH * NSA_HEADS]
    br = br.reshape(bd, tn, NSA_HEADS, N_NSA_BRANCH).transpose(0, 2, 1, 3).reshape(bd, rows, N_NSA_BRANCH)
    br = jnp.pad(br, ((0, 0), (0, 0), (0, LANES - N_NSA_BRANCH)))
    seq = lambda r, c, blk=0: pl.BlockSpec((1, r, c), lambda i, pt: (i, 0, blk))
    const2 = lambda r, c: pl.BlockSpec((r, c), lambda i, pt: (0, 0))
    grid_spec = pltpu.PrefetchScalarGridSpec(
        num_scalar_prefetch=1,
        grid=(bd,),
        in_specs=[pl.BlockSpec(memory_space=pl.ANY),
                  seq(rows, SB_WIDTH), seq(rows, LANES), seq(tn, KV_CH), seq(tn, WIN_CH), seq(n_buf, WIN_CH),
                  seq(tn, SB_WIDTH, 0), seq(tn, NSA_WIDTH, 1), seq(rows, LANES),
                  const2(2 * kb, kb), const2(ns_pad, ng),
                  pl.BlockSpec((2, CMP_BLOCK, LANES), lambda i, pt: (0, 0, 0))],
        out_specs=[seq(tn, SB_WIDTH), seq(tn, NSA_WIDTH), seq(n_buf, WIN_CH)],
        scratch_shapes=[pltpu.VMEM((2, past, KV_CH), F32), pltpu.VMEM((LANES, KV_CH), F32),
                        pltpu.VMEM((past, LANES), F32), pltpu.VMEM((ng + 8, LANES), F32),
                        pltpu.VMEM((rows, SB_WIDTH), F32), pltpu.VMEM((rows, LANES), F32),
                        pltpu.SemaphoreType.DMA((2,))])
    tap = jnp.pad(_tap_matrix_t(ns, ng), ((0, ns_pad - ns), (0, 0)))
    return pl.pallas_call(
        functools.partial(_sample_kernel, layer=layer, n_pages=n_pages, tn=tn, kb=kb),
        grid_spec=grid_spec,
        out_shape=[jax.ShapeDtypeStruct((bd, tn, SB_WIDTH), F32), jax.ShapeDtypeStruct((bd, tn, NSA_WIDTH), F32),
                   jax.ShapeDtypeStruct((bd, n_buf, WIN_CH), F32)],
        compiler_params=pltpu.CompilerParams(dimension_semantics=("arbitrary",), vmem_limit_bytes=VMEM_LIMIT),
        name="sample_attention",
    )(page_table, cache_kv, qsb, qn, kv.reshape(bd, tn, KV_CH), win.reshape(bd, tn, WIN_CH), state_win_l,
      gates3, gates3, br, _upper_ones2(kb), tap, wplane)


def _sample_layer(x, xb, ws, bs, l, page_table, cache_kv, state_win_l, wplane, w_sb, w_nsa, w_out, b_out,
                  ln_g, ln_b, alpha, bd, tn):
    kv, _, win, _, qp, gates = _project(xb, ws, bs, l, 1024)
    osb, onsa, win_out = _sample_attention(page_table, cache_kv, l, qp, kv, win, state_win_l, gates, wplane, tn)
    y, yb = _merge(osb.reshape(bd * tn, SB_WIDTH), onsa.reshape(bd * tn, NSA_WIDTH), gates, x,
                   w_sb, w_nsa, w_out, b_out, ln_g, ln_b, alpha, 512)
    return y, yb, kv.reshape(bd, tn, KV_CH), win_out


def kernel(x_prompt, x_sample, cache_kv, state_win, page_table, w_in, b_in, cmp_w, w_sb, w_nsa, w_out, b_out, ln_g, ln_b):
    depth = w_in.shape[0]
    alpha = float((2 * depth) ** 0.25)
    b, t, _ = x_prompt.shape
    bd, tn, _ = x_sample.shape
    ws, bs = _prepare_weights(w_in, b_in)
    xp = x_prompt.reshape(b * t, D_MODEL)
    xs = x_sample.reshape(bd * tn, D_MODEL)
    xpb, xsb = xp.astype(BF16), xs.astype(BF16)
    kv_p, kv_s, win_p, win_s = [], [], [], []
    for l in range(depth):
        shared = (_cmp_plane(cmp_w[l]), w_sb[l].astype(BF16), w_nsa[l].astype(BF16), w_out[l].astype(BF16),
                  b_out[l][None], ln_g[l][None], ln_b[l][None], alpha)
        xp, xpb, kvp, wp = _prompt_layer(xp, xpb, ws, bs, l, *shared, b, t)
        xs, xsb, kvs, wsm = _sample_layer(xs, xsb, ws, bs, l, page_table, cache_kv, state_win[l], *shared, bd, tn)
        kv_p.append(kvp)
        kv_s.append(kvs)
        win_p.append(wp)
        win_s.append(wsm)
    return (xp.reshape(b, t, D_MODEL), xs.reshape(bd, tn, D_MODEL), jnp.stack(kv_p), jnp.stack(kv_s),
            jnp.stack(win_p), jnp.stack(win_s))
```

```python
import functools

import numpy as np
import jax
import jax.numpy as jnp
from jax import lax
from jax.experimental import pallas as pl
from jax.experimental.pallas import tpu as pltpu

F32 = jnp.float32
BF16 = jnp.bfloat16

D_MODEL = 1024
HEAD_DIM = 64
SB_HEADS = 8
SB_WIDTH = SB_HEADS * HEAD_DIM
NSA_HEADS = 8
NSA_KV_HEADS = 2
NSA_GROUP = NSA_HEADS // NSA_KV_HEADS
NSA_WIDTH = NSA_HEADS * HEAD_DIM
NSA_KV_WIDTH = NSA_KV_HEADS * HEAD_DIM
N_NSA_BRANCH = 3
CMP_BLOCK = 32
CMP_STRIDE = 16
SEL_BLOCK = 64
SEL_TOPK = 8
WINDOW = 512
PAGE_SIZE = 128
LN_EPS = 1e-5

KV_CH = 2 * SB_WIDTH + 4 * NSA_KV_WIDTH
WIN_CH = 2 * NSA_KV_WIDTH
OFF_WIN = KV_CH
OFF_SB_Q = OFF_WIN + WIN_CH
OFF_NSA_Q = OFF_SB_Q + SB_WIDTH
OFF_SB_GATE = OFF_NSA_Q + NSA_WIDTH
OFF_NSA_GATE = OFF_SB_GATE + SB_WIDTH
OFF_BR_GATE = OFF_NSA_GATE + NSA_WIDTH
OFF_MERGE = OFF_BR_GATE + N_NSA_BRANCH * NSA_HEADS
IN_COLS = OFF_MERGE + 2 * D_MODEL

LANES = 128
Q_PLACED = SB_HEADS * LANES
GATE_COLS = SB_WIDTH + NSA_WIDTH + 2 * D_MODEL + LANES
GATE_BR_BLOCK = (SB_WIDTH + NSA_WIDTH + 2 * D_MODEL) // LANES
SCALE = HEAD_DIM ** -0.5
NEG = -1e30
LOG2E = float(np.log2(np.e))
MASK_BIAS = -2.0 ** 126
VMEM_LIMIT = 56 * 1024 * 1024

_NT = (((1,), (1,)), ((), ()))


def _slope(h):
    return float(2.0 ** (-8.0 * (h + 1) / NSA_HEADS))


def _dot(a, b):
    return jnp.dot(a, b, preferred_element_type=F32)


def _dot_nt(a, b):
    return lax.dot_general(a, b, _NT, preferred_element_type=F32)


def _split2(x):
    hi = x.astype(BF16)
    lo = (x - hi.astype(F32)).astype(BF16)
    return hi, lo


def _split3(x):
    hi = x.astype(BF16)
    r = x - hi.astype(F32)
    mid = r.astype(BF16)
    lo = (r - mid.astype(F32)).astype(BF16)
    return hi, mid, lo


def _sigmoid(x):
    return 1.0 / (1.0 + jnp.exp(-x))


def _masked_softmax(s, mask):
    s = jnp.where(mask, s, NEG)
    m = jnp.max(s, axis=-1, keepdims=True)
    e = jnp.where(mask, jnp.exp(s - m), 0.0)
    return e / jnp.maximum(jnp.sum(e, axis=-1, keepdims=True), 1e-30)


def _linear_kernel(x_ref, w_ref, b_ref, *out_refs):
    acc = _dot(x_ref[...], w_ref[...]) + b_ref[...]
    for o in out_refs:
        o[...] = acc.astype(o.dtype)


def _linear(x, w, b, out_dtypes, tm, tn):
    n, k = x.shape
    c = w.shape[1]
    tm = min(tm, n)
    assert n % tm == 0 and c % tn == 0
    return pl.pallas_call(
        _linear_kernel,
        grid=(n // tm, c // tn),
        in_specs=[pl.BlockSpec((tm, k), lambda i, j: (i, 0)),
                  pl.BlockSpec((k, tn), lambda i, j: (0, j)),
                  pl.BlockSpec((1, tn), lambda i, j: (0, j))],
        out_specs=[pl.BlockSpec((tm, tn), lambda i, j: (i, j)) for _ in out_dtypes],
        out_shape=[jax.ShapeDtypeStruct((n, c), dt) for dt in out_dtypes],
        compiler_params=pltpu.CompilerParams(dimension_semantics=("parallel", "parallel"),
                                             vmem_limit_bytes=VMEM_LIMIT),
        name="linear",
    )(x, w, b)


def _sb_scores(q, k):
    return _dot_nt(q, k) * (-LOG2E)


def _sb_log_weights(n, u2, mask):
    nabs = pltpu.bitcast(pltpu.bitcast(n, jnp.uint32) | jnp.uint32(0x80000000), F32)
    ls = jnp.minimum(n, 0.0) - jnp.log2(1.0 + jnp.exp2(nabs))
    if mask is not None:
        ls = jnp.where(mask, ls, 0.0)
    hi, lo = _split2(ls)
    rest = _dot(jnp.concatenate([hi, lo], axis=1), u2)
    return (ls - n) + rest, rest[:, 0:1] + ls[:, 0:1]


def _sb_tile(q, k, v, u2, carry, mask):
    x, tot = _sb_log_weights(_sb_scores(q, k), u2, mask)
    a = jnp.exp2(x + carry)
    if mask is not None:
        a = jnp.where(mask, a, 0.0)
    return _dot(a.astype(BF16), v), tot


def _sb_prompt_kernel(q_ref, k_ref, v_ref, g_ref, u_ref, o_ref, acc_ref, car_ref, *, tq, tk):
    i = pl.program_id(2)
    u2 = u_ref[...]
    n_diag = tq // tk
    j_last = (i + 1) * n_diag - 1
    reps = tk // LANES
    acc_ref[...] = jnp.zeros(acc_ref.shape, F32)
    car_ref[...] = jnp.zeros(car_ref.shape, F32)

    def tile_pair(j, r0, masked):
        st = pl.multiple_of(j * tk, tk)
        k = k_ref[0, pl.ds(st, tk), :]
        v = v_ref[0, pl.ds(st, tk), :]
        mask = None
        if masked:
            mask = (lax.broadcasted_iota(jnp.int32, (tq - r0, tk), 1)
                    < lax.broadcasted_iota(jnp.int32, (tq - r0, tk), 0))
        for hh in range(2):
            q = q_ref[0, r0:tq, hh * LANES:(hh + 1) * LANES]
            c1 = car_ref[hh, r0:tq, :]
            av, tot = _sb_tile(q, k, v, u2, jnp.concatenate([c1] * reps, axis=1), mask)
            acc_ref[hh, r0:tq, :] += av
            car_ref[hh, r0:tq, :] = c1 + tot

    for m in range(n_diag):
        tile_pair(j_last - m, (n_diag - 1 - m) * tk, True)

    def body(s, c):
        tile_pair(j_last - s, 0, False)
        return c

    lax.fori_loop(n_diag, j_last + 1, body, 0)
    lane = lax.broadcasted_iota(jnp.int32, (tq, LANES), 1)
    o = jnp.where(lane < HEAD_DIM, acc_ref[0], acc_ref[1])
    g = g_ref[0]
    o_ref[0] = (o * (g * _sigmoid(g))).astype(o_ref.dtype)


def _upper_ones2(t):
    r = np.arange(t)
    u = (r[:, None] > r[None, :]).astype(np.float32)
    return jnp.asarray(np.concatenate([u, u], axis=0), dtype=BF16)


def _sb_prompt(qp, kvb, gates, tq, tk):
    b, t, _ = qp.shape
    tq, tk = min(tq, t), min(tk, t)
    assert t % tq == 0 and tq % tk == 0
    hp = SB_HEADS // 2
    return pl.pallas_call(
        functools.partial(_sb_prompt_kernel, tq=tq, tk=tk),
        grid=(b, hp, t // tq),
        in_specs=[pl.BlockSpec((1, tq, 2 * LANES), lambda bi, h, i: (bi, i, h)),
                  pl.BlockSpec((1, t, LANES), lambda bi, h, i: (bi, 0, h)),
                  pl.BlockSpec((1, t, LANES), lambda bi, h, i: (bi, 0, hp + h)),
                  pl.BlockSpec((1, tq, LANES), lambda bi, h, i: (bi, i, h)),
                  pl.BlockSpec((2 * tk, tk), lambda bi, h, i: (0, 0))],
        out_specs=pl.BlockSpec((1, tq, LANES), lambda bi, h, i: (bi, i, h)),
        out_shape=jax.ShapeDtypeStruct((b, t, SB_WIDTH), BF16),
        scratch_shapes=[pltpu.VMEM((2, tq, LANES), F32), pltpu.VMEM((2, tq, LANES), F32)],
        compiler_params=pltpu.CompilerParams(dimension_semantics=("parallel", "parallel", "arbitrary"),
                                             vmem_limit_bytes=VMEM_LIMIT),
        name="sb_prompt",
    )(qp, kvb, kvb, gates, _upper_ones2(tk))


def _compress_kernel(k_ref, v_ref, w_ref, o_ref, sh_ref, *, ng):
    for c, x_ref in enumerate((k_ref, v_ref)):
        a = jnp.zeros((ng, LANES), F32)
        bh = jnp.zeros((ng, LANES), F32)
        for p in range(CMP_STRIDE):
            xp = x_ref[0, pl.ds(p, ng, stride=CMP_STRIDE), :]
            a = a + xp * w_ref[c, p:p + 1, :]
            bh = bh + xp * w_ref[c, CMP_STRIDE + p:CMP_STRIDE + p + 1, :]
        sh_ref[0:ng, :] = bh
        sh_ref[ng:ng + 8, :] = jnp.zeros((8, LANES), F32)
        o_ref[0, :, c * LANES:(c + 1) * LANES] = (a + sh_ref[1:ng + 1, :]).astype(o_ref.dtype)


def _cmp_plane(cmp_w_l):
    return jnp.broadcast_to(cmp_w_l[:, :, None], (2, CMP_BLOCK, LANES))


def _compress(kv, wplane):
    b, l, _ = kv.shape
    assert CMP_BLOCK == 2 * CMP_STRIDE
    ng = l // CMP_STRIDE
    cmp_k_blk = 2 * SB_WIDTH // LANES
    return pl.pallas_call(
        functools.partial(_compress_kernel, ng=ng),
        grid=(b,),
        in_specs=[pl.BlockSpec((1, l, LANES), lambda bi: (bi, 0, cmp_k_blk)),
                  pl.BlockSpec((1, l, LANES), lambda bi: (bi, 0, cmp_k_blk + 1)),
                  pl.BlockSpec((2, CMP_BLOCK, LANES), lambda bi: (0, 0, 0))],
        out_specs=pl.BlockSpec((1, ng, 2 * LANES), lambda bi: (bi, 0, 0)),
        out_shape=jax.ShapeDtypeStruct((b, ng, 2 * LANES), BF16),
        scratch_shapes=[pltpu.VMEM((ng + 8, LANES), F32)],
        compiler_params=pltpu.CompilerParams(dimension_semantics=("parallel",), vmem_limit_bytes=VMEM_LIMIT),
        name="compress",
    )(kv, kv, wplane)


def _tap_matrix_t(ns, nc_pad):
    ratio = SEL_BLOCK // CMP_STRIDE
    taps = np.convolve(np.ones(ratio), np.ones(CMP_BLOCK // CMP_STRIDE))
    m = np.zeros((ns, nc_pad), np.float32)
    for j in range(ns):
        for i, tv in enumerate(taps):
            if ratio * j + i < nc_pad:
                m[j, ratio * j + i] = tv
    return jnp.asarray(m, dtype=BF16)


def _select_t(pslc_t, cur_t, ns):
    shape = pslc_t.shape
    blk = lax.broadcasted_iota(jnp.int32, shape, 0)
    forced = (blk == 0) | (blk == cur_t) | (blk == cur_t - 1)
    score = jnp.where((blk > cur_t) | (blk >= ns), -jnp.inf, jnp.where(forced, jnp.inf, pslc_t))
    blk_f = blk.astype(F32)
    sel = jnp.zeros(shape, F32)
    for _ in range(SEL_TOPK):
        top = jnp.max(score, axis=0, keepdims=True)
        first = jnp.min(jnp.where(score == top, blk_f, float(shape[0])), axis=0, keepdims=True)
        pick = blk_f == first
        sel = jnp.where(pick, 1.0, sel)
        score = jnp.where(pick, -jnp.inf, score)
    return jnp.where(blk < ns, sel, 0.0)


def _pos_features(pos, block_lanes):
    pos = np.asarray(pos)
    ok = pos >= 0
    p = np.where(ok, pos, 0)
    f = np.zeros((len(pos), LANES), np.float32)
    f[:, 0] = ok
    f[:, 1] = ok
    f[:, 2] = p // SEL_BLOCK
    f[:, 3] = p % SEL_BLOCK
    if block_lanes:
        f[:, 4:7] = 1.0
        f[np.arange(len(pos)), SEL_BLOCK + p // SEL_BLOCK] = 1.0
    return jnp.asarray(f, dtype=BF16)


def _values_and_ones(v):
    own = np.arange(LANES) // HEAD_DIM
    return jnp.concatenate([jnp.where(jnp.asarray(own == g), v, jnp.ones_like(v)) for g in range(NSA_KV_HEADS)],
                           axis=-1)


def _nsa_prompt_kernel(q_ref, kc_ref, sk_ref, sv_ref, w_ref, br_ref, g_ref, tap_ref, o_ref,
                       comb_ref, qx_ref, m_ref, acc_ref, *, tq, ts, tk, nc, ns):
    i = pl.program_id(1)
    nc_pad = kc_ref.shape[1]
    n_diag = tq // tk
    j_last = (i + 1) * n_diag - 1
    lane_s = lax.broadcasted_iota(jnp.int32, (ts, LANES), 1)
    lane_q = lax.broadcasted_iota(jnp.int32, (tq, LANES), 1)
    kc = kc_ref[0, :, 0:2 * LANES]
    vc = kc_ref[0, :, 2 * LANES:3 * LANES]
    n_idx = lax.broadcasted_iota(jnp.int32, (1, nc_pad), 1)
    c_end = n_idx * CMP_STRIDE + (CMP_BLOCK - 1)
    wlen = WINDOW + ts
    wd = (lax.broadcasted_iota(jnp.int32, (ts, wlen), 0) + WINDOW
          - lax.broadcasted_iota(jnp.int32, (ts, wlen), 1))
    tap = tap_ref[...]

    def sub_block(sb, c):
        r0 = pl.multiple_of(sb * ts, ts)
        start = i * tq + r0
        qpos = start + lax.broadcasted_iota(jnp.int32, (ts, 1), 0)
        qpos_t = start + lax.broadcasted_iota(jnp.int32, (1, ts), 1)
        br = _sigmoid(br_ref[0, pl.ds(r0, ts), :])
        cmask = (c_end <= qpos) & (n_idx < nc)
        ws = pl.multiple_of(start, ts)
        kw = w_ref[0, pl.ds(ws, wlen), 0:2 * LANES]
        wpos = start - WINDOW + lax.broadcasted_iota(jnp.int32, (1, wlen), 1)
        wmask = (wd >= 0) & (wd <= WINDOW) & (wpos >= 0)
        qa = (qpos // SEL_BLOCK).astype(F32)
        qb = (qpos % SEL_BLOCK).astype(F32)

        def alibi_lanes(h):
            sl = _slope(h)
            return jnp.where(lane_s == 0, -sl * SEL_BLOCK * qa,
                             jnp.where(lane_s == 1, -sl * qb,
                                       jnp.where(lane_s == 2, sl * SEL_BLOCK, jnp.where(lane_s == 3, sl, 0.0))))

        stack = lambda x: jnp.concatenate([x] * NSA_GROUP, axis=0)
        cmask4, wmask4 = stack(cmask), stack(wmask)
        for g in range(NSA_KV_HEADS):
            heads = range(g * NSA_GROUP, (g + 1) * NSA_GROUP)
            q4 = jnp.concatenate(
                [jnp.concatenate([q_ref[0, pl.ds(r0, ts), h * LANES:(h + 1) * LANES], alibi_lanes(h).astype(BF16)],
                                 axis=1) for h in heads], axis=0)
            p4 = _masked_softmax(_dot_nt(q4, kc), cmask4)
            imp = p4[0:ts]
            for r in range(1, NSA_GROUP):
                imp = imp + p4[r * ts:(r + 1) * ts]
            o_cmp = _dot(p4.astype(BF16), vc)
            s4 = jnp.where(wmask4, _dot_nt(q4, kw), MASK_BIAS)
            e4 = jnp.exp(s4 - jnp.max(s4, axis=1, keepdims=True))
            ow = _dot(e4.astype(BF16), w_ref[0, pl.ds(ws, wlen), (2 + g) * LANES:(3 + g) * LANES])
            o_win = ow / jnp.maximum(pltpu.roll(ow, HEAD_DIM, axis=1), 1e-30)
            for r, h in enumerate(heads):
                c0 = N_NSA_BRANCH * h
                comb_ref[h, pl.ds(r0, ts), :] = (br[:, c0:c0 + 1] * o_cmp[r * ts:(r + 1) * ts]
                                                 + br[:, c0 + 2:c0 + 3] * o_win[r * ts:(r + 1) * ts])
            ih, im, il = _split3(imp)
            pslc_t = _dot_nt(tap, ih) + _dot_nt(tap, im) + _dot_nt(tap, il)
            sel_t = _select_t(pslc_t, qpos_t // SEL_BLOCK, ns)
            sel = jnp.concatenate([jnp.zeros_like(sel_t), sel_t], axis=0).T
            bias = jnp.where(sel > 0.5, 0.0, MASK_BIAS)
            for r in range(NSA_GROUP):
                h = g * NSA_GROUP + r
                qx_ref[h, pl.ds(r0, ts), :] = jnp.where(lane_s >= SEL_BLOCK, bias, alibi_lanes(h)).astype(BF16)
        return c

    lax.fori_loop(0, tq // ts, sub_block, 0)

    def tiles(fn):
        for m in range(n_diag):
            fn(j_last - m, (n_diag - 1 - m) * tk, True)

        def body(s, c):
            fn(j_last - s, 0, False)
            return c

        lax.fori_loop(n_diag, j_last + 1, body, 0)

    def scores(h, j, r0, masked):
        ks = pl.multiple_of(j * tk, tk)
        qa = jnp.concatenate([q_ref[0, r0:tq, h * LANES:(h + 1) * LANES], qx_ref[h, r0:tq, :]], axis=1)
        s = _dot_nt(qa, sk_ref[0, pl.ds(ks, tk), :])
        if masked:
            s = jnp.where(lax.broadcasted_iota(jnp.int32, (tq - r0, tk), 1)
                          <= lax.broadcasted_iota(jnp.int32, (tq - r0, tk), 0), s, MASK_BIAS)
        return s

    m_ref[...] = jnp.full(m_ref.shape, MASK_BIAS, F32)

    def max_tile(j, r0, masked):
        for h in range(NSA_HEADS):
            s = scores(h, j, r0, masked)
            sm = s[:, 0:LANES]
            for c in range(1, tk // LANES):
                sm = jnp.maximum(sm, s[:, c * LANES:(c + 1) * LANES])
            m_ref[h, r0:tq, :] = jnp.maximum(m_ref[h, r0:tq, :], sm)

    tiles(max_tile)
    for h in range(NSA_HEADS):
        neg_m = -jnp.max(m_ref[h], axis=1, keepdims=True)
        mh, mm, ml = _split3(neg_m)
        x = qx_ref[h].astype(F32)
        x = jnp.where(lane_q == 4, mh.astype(F32), jnp.where(lane_q == 5, mm.astype(F32),
                                                             jnp.where(lane_q == 6, ml.astype(F32), x)))
        qx_ref[h] = x.astype(BF16)
    acc_ref[...] = jnp.zeros(acc_ref.shape, F32)

    def sum_tile(j, r0, masked):
        ks = pl.multiple_of(j * tk, tk)
        for h in range(NSA_HEADS):
            g = h // NSA_GROUP
            e = jnp.exp(scores(h, j, r0, masked))
            acc_ref[h, r0:tq, :] += _dot(e.astype(BF16), sv_ref[0, pl.ds(ks, tk), g * LANES:(g + 1) * LANES])

    tiles(sum_tile)

    br = _sigmoid(br_ref[0])
    for pair in range(NSA_HEADS // 2):
        outs = []
        for h in (2 * pair, 2 * pair + 1):
            acc = acc_ref[h]
            o_slc = acc / jnp.maximum(pltpu.roll(acc, HEAD_DIM, axis=1), 1e-30)
            c1 = N_NSA_BRANCH * h + 1
            outs.append(comb_ref[h] + br[:, c1:c1 + 1] * o_slc)
        lo, hi = outs
        if (2 * pair) // NSA_GROUP == 0:
            hi = pltpu.roll(hi, HEAD_DIM, axis=1)
        else:
            lo = pltpu.roll(lo, HEAD_DIM, axis=1)
        gt = g_ref[0, :, pair * LANES:(pair + 1) * LANES]
        o_ref[0, :, pair * LANES:(pair + 1) * LANES] = (
            jnp.where(lane_q < HEAD_DIM, lo, hi) * (gt * _sigmoid(gt))).astype(o_ref.dtype)


def _nsa_prompt(qp, kvb, kcb, winb, gates, tq, ts, tk):
    b, t, _ = qp.shape
    tq, ts, tk = min(tq, t), min(ts, t), min(tk, t)
    nc_pad = kcb.shape[1]
    nc = t // CMP_STRIDE - CMP_BLOCK // CMP_STRIDE + 1
    ns = t // SEL_BLOCK
    assert t % tq == 0 and tq % tk == 0 and tq % ts == 0 and ns <= SEL_BLOCK and tk % SEL_BLOCK == 0
    bcast = lambda f: jnp.broadcast_to(f, (b,) + f.shape)
    slc = (2 * SB_WIDTH + 2 * NSA_KV_WIDTH)
    k_aug = jnp.concatenate([kvb[:, :, slc:slc + LANES], bcast(_pos_features(np.arange(t), True))], axis=-1)
    v_aug = _values_and_ones(kvb[:, :, slc + LANES:slc + 2 * LANES])
    c_end = np.arange(nc_pad) * CMP_STRIDE + (CMP_BLOCK - 1)
    kc_in = jnp.concatenate([kcb[:, :, 0:LANES], bcast(_pos_features(c_end, False)), kcb[:, :, LANES:2 * LANES]],
                            axis=-1)
    wpad = jnp.pad(winb, ((0, 0), (WINDOW, 0), (0, 0)))
    w_in = jnp.concatenate([wpad[:, :, 0:LANES], bcast(_pos_features(np.arange(t + WINDOW) - WINDOW, False)),
                            _values_and_ones(wpad[:, :, LANES:2 * LANES])], axis=-1)
    tap = jnp.pad(_tap_matrix_t(ns, nc_pad), ((0, SEL_BLOCK - ns), (0, 0)))
    return pl.pallas_call(
        functools.partial(_nsa_prompt_kernel, tq=tq, ts=ts, tk=tk, nc=nc, ns=ns),
        grid=(b, t // tq),
        in_specs=[pl.BlockSpec((1, tq, Q_PLACED), lambda bi, i: (bi, i, 1)),
                  pl.BlockSpec((1, nc_pad, 3 * LANES), lambda bi, i: (bi, 0, 0)),
                  pl.BlockSpec((1, t, 2 * LANES), lambda bi, i: (bi, 0, 0)),
                  pl.BlockSpec((1, t, NSA_KV_HEADS * LANES), lambda bi, i: (bi, 0, 0)),
                  pl.BlockSpec((1, t + WINDOW, 4 * LANES), lambda bi, i: (bi, 0, 0)),
                  pl.BlockSpec((1, tq, LANES), lambda bi, i: (bi, i, GATE_BR_BLOCK)),
                  pl.BlockSpec((1, tq, NSA_WIDTH), lambda bi, i: (bi, i, 1)),
                  pl.BlockSpec((SEL_BLOCK, nc_pad), lambda bi, i: (0, 0))],
        out_specs=pl.BlockSpec((1, tq, NSA_WIDTH), lambda bi, i: (bi, i, 0)),
        out_shape=jax.ShapeDtypeStruct((b, t, NSA_WIDTH), BF16),
        scratch_shapes=[pltpu.VMEM((NSA_HEADS, tq, LANES), F32), pltpu.VMEM((NSA_HEADS, tq, LANES), BF16),
                        pltpu.VMEM((NSA_HEADS, tq, LANES), F32), pltpu.VMEM((NSA_HEADS, tq, LANES), F32)],
        compiler_params=pltpu.CompilerParams(dimension_semantics=("parallel", "arbitrary"),
                                             vmem_limit_bytes=VMEM_LIMIT),
        name="nsa_prompt",
    )(qp, kc_in, k_aug, v_aug, w_in, gates, gates, tap)


def _merge_kernel(osb_ref, onsa_ref, gsb_ref, gnsa_ref, x_ref, wsb_ref, wnsa_ref, wout_ref, bout_ref,
                  lng_ref, lnb_ref, y_ref, yb_ref, *, alpha):
    u_sb = _dot(osb_ref[...].astype(BF16), wsb_ref[...])
    u_nsa = _dot(onsa_ref[...].astype(BF16), wnsa_ref[...])
    mid = _sigmoid(gsb_ref[...]) * u_sb + _sigmoid(gnsa_ref[...]) * u_nsa
    h = _dot(mid.astype(BF16), wout_ref[...]) + bout_ref[...]
    xf = alpha * x_ref[...] + h
    mu = jnp.mean(xf, axis=-1, keepdims=True)
    xc = xf - mu
    var = jnp.mean(xc * xc, axis=-1, keepdims=True)
    y = xc * lax.rsqrt(var + LN_EPS) * lng_ref[...] + lnb_ref[...]
    y_ref[...] = y
    yb_ref[...] = y.astype(BF16)


def _merge(osb, onsa, gates, x, w_sb, w_nsa, w_out, b_out, ln_g, ln_b, alpha, tm):
    n = x.shape[0]
    tm = min(tm, n)
    row = lambda c: pl.BlockSpec((tm, c), lambda i: (i, 0))
    full = lambda r, c: pl.BlockSpec((r, c), lambda i: (0, 0))
    merge_blk = (SB_WIDTH + NSA_WIDTH) // D_MODEL
    return pl.pallas_call(
        functools.partial(_merge_kernel, alpha=alpha),
        grid=(n // tm,),
        in_specs=[row(SB_WIDTH), row(NSA_WIDTH),
                  pl.BlockSpec((tm, D_MODEL), lambda i: (i, merge_blk)),
                  pl.BlockSpec((tm, D_MODEL), lambda i: (i, merge_blk + 1)),
                  row(D_MODEL), full(SB_WIDTH, D_MODEL), full(NSA_WIDTH, D_MODEL), full(D_MODEL, D_MODEL),
                  full(1, D_MODEL), full(1, D_MODEL), full(1, D_MODEL)],
        out_specs=[row(D_MODEL), row(D_MODEL)],
        out_shape=[jax.ShapeDtypeStruct((n, D_MODEL), F32), jax.ShapeDtypeStruct((n, D_MODEL), BF16)],
        compiler_params=pltpu.CompilerParams(dimension_semantics=("parallel",), vmem_limit_bytes=VMEM_LIMIT),
        name="merge",
    )(osb, onsa, gates, gates, x, w_sb, w_nsa, w_out, b_out, ln_g, ln_b)


def _place_heads(w, tile_of_head):
    lead = w.shape[:-1]
    onehot = np.zeros((SB_HEADS, 2), np.float32)
    onehot[np.arange(SB_HEADS), tile_of_head] = 1.0
    w = w.reshape(lead + (SB_HEADS, 1, HEAD_DIM)) * jnp.asarray(onehot)[:, :, None]
    return w.reshape(lead + (SB_HEADS * LANES,))


def _prepare_weights(w_in, b_in):
    def sections(a):
        sb_q = _place_heads(a[..., OFF_SB_Q:OFF_SB_Q + SB_WIDTH] * SCALE, np.arange(SB_HEADS) % 2)
        nsa_q = _place_heads(a[..., OFF_NSA_Q:OFF_NSA_Q + NSA_WIDTH] * SCALE, np.arange(NSA_HEADS) // NSA_GROUP)
        br = a[..., OFF_BR_GATE:OFF_MERGE]
        br = jnp.pad(br, [(0, 0)] * (a.ndim - 1) + [(0, LANES - br.shape[-1])])
        gates = jnp.concatenate([a[..., OFF_SB_GATE:OFF_BR_GATE], a[..., OFF_MERGE:], br], axis=-1)
        return a[..., :KV_CH], a[..., OFF_WIN:OFF_WIN + WIN_CH], jnp.concatenate([sb_q, nsa_q], axis=-1), gates
    ws = [s.astype(BF16) for s in sections(w_in)]
    bs = [s[:, None, :] for s in sections(b_in)]
    return ws, bs


def _project(xb, ws, bs, l, tm):
    kv, kvb = _linear(xb, ws[0][l], bs[0][l], (F32, BF16), tm, 512)
    win, winb = _linear(xb, ws[1][l], bs[1][l], (F32, BF16), tm, WIN_CH)
    (qp,) = _linear(xb, ws[2][l], bs[2][l], (BF16,), tm, 512)
    (gates,) = _linear(xb, ws[3][l], bs[3][l], (F32,), tm, 640)
    return kv, kvb, win, winb, qp, gates


def _prompt_layer(x, xb, ws, bs, l, wplane, w_sb, w_nsa, w_out, b_out, ln_g, ln_b, alpha, b, t):
    kv, kvb, win, winb, qp, gates = _project(xb, ws, bs, l, 1024)
    to3 = lambda a: a.reshape(b, t, a.shape[-1])
    kv3, kvb3, qp3, gates3 = to3(kv), to3(kvb), to3(qp), to3(gates)
    osb = _sb_prompt(qp3, kvb3, gates3, 1024, 256)
    kcb = _compress(kv3, wplane)
    onsa = _nsa_prompt(qp3, kvb3, kcb, to3(winb), gates3, 512, 128, 256)
    y, yb = _merge(osb.reshape(b * t, SB_WIDTH), onsa.reshape(b * t, NSA_WIDTH), gates, x,
                   w_sb, w_nsa, w_out, b_out, ln_g, ln_b, alpha, 512)
    return y, yb, kv3, to3(win)[:, -min(WINDOW, t):]


def _sample_kernel(pt_ref, cache_ref, qsb_ref, qn_ref, kvn_ref, winn_ref, winb_ref, gsb_ref, gnsa_ref, br_ref,
                   u_ref, tap_ref, wcmp_ref, osb_ref, onsa_ref, wout_ref,
                   buf, newbuf, cmpbuf, sh_ref, sem, *, layer, n_pages, tn, kb):
    b = pl.program_id(0)
    nb = pl.num_programs(0)
    slot = b % 2
    past = n_pages * PAGE_SIZE
    rows = SB_HEADS * tn
    n_buf = winb_ref.shape[1]

    def page_copy(seq, p, slot_):
        return pltpu.make_async_copy(cache_ref.at[layer, pt_ref[seq, p]],
                                     buf.at[slot_, pl.ds(p * PAGE_SIZE, PAGE_SIZE), :], sem.at[slot_])

    @pl.when(b == 0)
    def _():
        for p in range(n_pages):
            page_copy(0, p, 0).start()

    @pl.when(b + 1 < nb)
    def _():
        for p in range(n_pages):
            page_copy(b + 1, p, 1 - slot).start()

    for p in range(n_pages):
        page_copy(b, p, slot).wait()

    row = lax.broadcasted_iota(jnp.int32, (rows, 1), 0)
    tok = row % tn
    qpos = past + tok
    newbuf[...] = jnp.zeros(newbuf.shape, F32)
    newbuf[0:tn, :] = kvn_ref[0]
    ncol = lax.broadcasted_iota(jnp.int32, (1, LANES), 1)

    qbd = qsb_ref[0]
    u2 = u_ref[...]
    un = u2[0:LANES, 0:LANES]
    mask_new = ncol < tok
    blocks = [past - (s + 1) * kb for s in range(past // kb)]
    ns_ = [_sb_scores(qbd, newbuf[:, 0:SB_WIDTH].astype(BF16))]
    ns_ += [_sb_scores(qbd, buf[slot, ks:ks + kb, 0:SB_WIDTH].astype(BF16)) for ks in blocks]
    lw = [_sb_log_weights(ns_[0], jnp.concatenate([un, un], axis=0), mask_new)]
    lw += [_sb_log_weights(n, u2, None) for n in ns_[1:]]
    a_new = jnp.where(mask_new, jnp.exp2(lw[0][0]), 0.0)
    acc = _dot(a_new.astype(BF16), newbuf[:, SB_WIDTH:2 * SB_WIDTH].astype(BF16))
    carry = lw[0][1]
    for (x, tot), ks in zip(lw[1:], blocks):
        a = jnp.exp2(x + carry)
        acc = acc + _dot(a.astype(BF16), buf[slot, ks:ks + kb, SB_WIDTH:2 * SB_WIDTH].astype(BF16))
        carry = carry + tot
    hrow = lax.broadcasted_iota(jnp.int32, (rows, SB_WIDTH), 0) // tn
    hcol = lax.broadcasted_iota(jnp.int32, (rows, SB_WIDTH), 1) // HEAD_DIM
    own_head = hrow == hcol

    def fold(x):
        return jnp.sum(jnp.where(own_head, x, 0.0).reshape(SB_HEADS, tn, SB_WIDTH), axis=0)

    gs = gsb_ref[0]
    osb_ref[0] = fold(acc) * (gs * _sigmoid(gs))

    qn = qn_ref[0]
    slope = jnp.exp2(-((row // tn) + 1).astype(F32) * (8.0 / NSA_HEADS))
    brs = _sigmoid(br_ref[0])

    ng = past // CMP_STRIDE
    off = 2 * SB_WIDTH
    kvc = []
    for c in range(2):
        cmpbuf[...] = buf[slot, :, off + c * LANES:off + (c + 1) * LANES]
        a_ = jnp.zeros((ng, LANES), F32)
        bh = jnp.zeros((ng, LANES), F32)
        for p in range(CMP_STRIDE):
            xp = cmpbuf[pl.ds(p, ng, stride=CMP_STRIDE), :]
            a_ = a_ + xp * wcmp_ref[c, p:p + 1, :]
            bh = bh + xp * wcmp_ref[c, CMP_STRIDE + p:CMP_STRIDE + p + 1, :]
        sh_ref[0:ng, :] = bh
        sh_ref[ng:ng + 8, :] = jnp.zeros((8, LANES), F32)
        kvc.append((a_ + sh_ref[1:ng + 1, :]).astype(BF16))
    kc, vc = kvc
    n_idx = lax.broadcasted_iota(jnp.int32, (1, ng), 1)
    c_end = n_idx * CMP_STRIDE + (CMP_BLOCK - 1)
    cmask = (c_end <= qpos) & (c_end < past)
    p_cmp = _masked_softmax(_dot_nt(qn, kc) - slope * (qpos - c_end).astype(F32), cmask)
    o_cmp = _dot(p_cmp.astype(BF16), vc)

    ns = tap_ref.shape[0]
    imp = jnp.sum(p_cmp.reshape(NSA_KV_HEADS, NSA_GROUP, tn, ng), axis=1).reshape(NSA_KV_HEADS * tn, ng)
    imp = jnp.concatenate([imp, jnp.zeros((LANES - NSA_KV_HEADS * tn, ng), F32)], axis=0)
    ih, im, il = _split3(imp)
    tap = tap_ref[...]
    pslc_t = _dot_nt(tap, ih) + _dot_nt(tap, im) + _dot_nt(tap, il)
    cur_t = (past + lax.broadcasted_iota(jnp.int32, (1, LANES), 1) % tn) // SEL_BLOCK
    n_sel = (past + tn + SEL_BLOCK - 1) // SEL_BLOCK
    sel = _select_t(pslc_t, cur_t, n_sel).T
    sel_rows = jnp.concatenate([sel[g * tn:(g + 1) * tn] for g in range(NSA_KV_HEADS) for _ in range(NSA_GROUP)],
                               axis=0).astype(BF16)
    ej = lax.broadcasted_iota(jnp.int32, (ns, past), 0)
    ec = lax.broadcasted_iota(jnp.int32, (ns, past), 1) // SEL_BLOCK
    valid_p = _dot(sel_rows, (ej == ec).astype(BF16)) > 0.5

    off = 2 * SB_WIDTH + 2 * NSA_KV_WIDTH
    pcol = lax.broadcasted_iota(jnp.int32, (1, past), 1)
    s_p = jnp.where(valid_p, _dot_nt(qn, buf[slot, :, off:off + LANES].astype(BF16))
                    - slope * (qpos - pcol).astype(F32), NEG)
    valid_n = ncol <= tok
    s_n = jnp.where(valid_n, _dot_nt(qn, newbuf[:, off:off + LANES].astype(BF16))
                    - slope * (tok - ncol).astype(F32), NEG)
    m = jnp.maximum(jnp.max(s_p, axis=1, keepdims=True), jnp.max(s_n, axis=1, keepdims=True))
    e_p = jnp.where(valid_p, jnp.exp(s_p - m), 0.0)
    e_n = jnp.where(valid_n, jnp.exp(s_n - m), 0.0)
    inv = 1.0 / jnp.maximum(jnp.sum(e_p, axis=1, keepdims=True) + jnp.sum(e_n, axis=1, keepdims=True), 1e-30)
    o_slc = (_dot((e_p * inv).astype(BF16), buf[slot, :, off + LANES:off + 2 * LANES].astype(BF16))
             + _dot((e_n * inv).astype(BF16), newbuf[:, off + LANES:off + 2 * LANES].astype(BF16)))

    winn = jnp.concatenate([winn_ref[0], jnp.zeros((LANES - tn, WIN_CH), F32)], axis=0)
    wcol = lax.broadcasted_iota(jnp.int32, (1, n_buf), 1)
    wd = tok + n_buf - wcol
    valid_w = (wd >= 0) & (wd <= WINDOW) & (past - n_buf + wcol >= 0)
    s_w = jnp.where(valid_w, _dot_nt(qn, winb_ref[0, :, 0:LANES].astype(BF16)) - slope * wd.astype(F32), NEG)
    s_n = jnp.where(valid_n, _dot_nt(qn, winn[:, 0:LANES].astype(BF16)) - slope * (tok - ncol).astype(F32), NEG)
    m = jnp.maximum(jnp.max(s_w, axis=1, keepdims=True), jnp.max(s_n, axis=1, keepdims=True))
    e_w = jnp.where(valid_w, jnp.exp(s_w - m), 0.0)
    e_n = jnp.where(valid_n, jnp.exp(s_n - m), 0.0)
    inv = 1.0 / jnp.maximum(jnp.sum(e_w, axis=1, keepdims=True) + jnp.sum(e_n, axis=1, keepdims=True), 1e-30)
    o_win = (_dot((e_w * inv).astype(BF16), winb_ref[0, :, LANES:2 * LANES].astype(BF16))
             + _dot((e_n * inv).astype(BF16), winn[:, LANES:2 * LANES].astype(BF16)))

    o_rows = brs[:, 0:1] * o_cmp + brs[:, 1:2] * o_slc + brs[:, 2:3] * o_win
    lane = lax.broadcasted_iota(jnp.int32, (rows, LANES), 1)
    rolled = pltpu.roll(o_rows, HEAD_DIM, axis=1)
    g0 = jnp.where(lane < HEAD_DIM, o_rows, rolled)
    g1 = jnp.where(lane < HEAD_DIM, rolled, o_rows)
    tiles = [g0] * (NSA_GROUP // 2) + [g1] * (NSA_GROUP // 2)
    gn = gnsa_ref[0]
    onsa_ref[0] = fold(jnp.concatenate(tiles, axis=1)) * (gn * _sigmoid(gn))

    wout_ref[0, 0:n_buf - tn, :] = winb_ref[0, tn:n_buf, :]
    wout_ref[0, n_buf - tn:n_buf, :] = winn_ref[0]


def _sample_attention(page_table, cache_kv, layer, qp, kv, win, state_win_l, gates, wplane, tn):
    bd, n_pages = page_table.shape
    past = n_pages * PAGE_SIZE
    n_buf = state_win_l.shape[1]
    rows = SB_HEADS * tn
    kb = 256
    assert NSA_KV_HEADS == 2 and SEL_TOPK >= 3 and past % kb == 0 and tn == 8 and n_buf >= tn
    assert past % SEL_BLOCK == 0 and tn <= SEL_BLOCK
    ng = past // CMP_STRIDE
    ns = -(-(past + tn) // SEL_BLOCK)
    ns_pad = -(-ns // SEL_BLOCK) * SEL_BLOCK
    assert past % CMP_STRIDE == 0 and tn < CMP_STRIDE
    q4 = qp.reshape(bd, tn, 2, SB_HEADS, LANES).transpose(0, 2, 3, 1, 4)
    col_tile = np.zeros((SB_HEADS, SB_WIDTH // LANES), np.float32)
    col_tile[np.arange(SB_HEADS), np.arange(SB_HEADS) // 2] = 1.0
    qsb = (q4[:, 0, :, :, None, :] * jnp.asarray(col_tile, BF16)[None, :, None, :, None]).reshape(bd, rows, SB_WIDTH)
    qn = q4[:, 1].reshape(bd, rows, LANES)
    gates3 = gates.reshape(bd, tn, GATE_COLS)
    br = gates3[:, :, GATE_BR_BLOCK * LANES:GATE_BR_BLOCK * LANES + N_NSA_BRANCH * NSA_HEADS]
    br = br.reshape(bd, tn, NSA_HEADS, N_NSA_BRANCH).transpose(0, 2, 1, 3).reshape(bd, rows, N_NSA_BRANCH)
    br = jnp.pad(br, ((0, 0), (0, 0), (0, LANES - N_NSA_BRANCH)))
    seq = lambda r, c, blk=0: pl.BlockSpec((1, r, c), lambda i, pt: (i, 0, blk))
    const2 = lambda r, c: pl.BlockSpec((r, c), lambda i, pt: (0, 0))
    grid_spec = pltpu.PrefetchScalarGridSpec(
        num_scalar_prefetch=1,
        grid=(bd,),
        in_specs=[pl.BlockSpec(memory_space=pl.ANY),
                  seq(rows, SB_WIDTH), seq(rows, LANES), seq(tn, KV_CH), seq(tn, WIN_CH), seq(n_buf, WIN_CH),
                  seq(tn, SB_WIDTH, 0), seq(tn, NSA_WIDTH, 1), seq(rows, LANES),
                  const2(2 * kb, kb), const2(ns_pad, ng),
                  pl.BlockSpec((2, CMP_BLOCK, LANES), lambda i, pt: (0, 0, 0))],
        out_specs=[seq(tn, SB_WIDTH), seq(tn, NSA_WIDTH), seq(n_buf, WIN_CH)],
        scratch_shapes=[pltpu.VMEM((2, past, KV_CH), F32), pltpu.VMEM((LANES, KV_CH), F32),
                        pltpu.VMEM((past, LANES), F32), pltpu.VMEM((ng + 8, LANES), F32),
                        pltpu.SemaphoreType.DMA((2,))])
    tap = jnp.pad(_tap_matrix_t(ns, ng), ((0, ns_pad - ns), (0, 0)))
    return pl.pallas_call(
        functools.partial(_sample_kernel, layer=layer, n_pages=n_pages, tn=tn, kb=kb),
        grid_spec=grid_spec,
        out_shape=[jax.ShapeDtypeStruct((bd, tn, SB_WIDTH), F32), jax.ShapeDtypeStruct((bd, tn, NSA_WIDTH), F32),
                   jax.ShapeDtypeStruct((bd, n_buf, WIN_CH), F32)],
        compiler_params=pltpu.CompilerParams(dimension_semantics=("arbitrary",), vmem_limit_bytes=VMEM_LIMIT),
        name="sample_attention",
    )(page_table, cache_kv, qsb, qn, kv.reshape(bd, tn, KV_CH), win.reshape(bd, tn, WIN_CH), state_win_l,
      gates3, gates3, br, _upper_ones2(kb), tap, wplane)


def _sample_layer(x, xb, ws, bs, l, page_table, cache_kv, state_win_l, wplane, w_sb, w_nsa, w_out, b_out,
                  ln_g, ln_b, alpha, bd, tn):
    kv, _, win, _, qp, gates = _project(xb, ws, bs, l, 1024)
    osb, onsa, win_out = _sample_attention(page_table, cache_kv, l, qp, kv, win, state_win_l, gates, wplane, tn)
    y, yb = _merge(osb.reshape(bd * tn, SB_WIDTH), onsa.reshape(bd * tn, NSA_WIDTH), gates, x,
                   w_sb, w_nsa, w_out, b_out, ln_g, ln_b, alpha, 512)
    return y, yb, kv.reshape(bd, tn, KV_CH), win_out


def kernel(x_prompt, x_sample, cache_kv, state_win, page_table, w_in, b_in, cmp_w, w_sb, w_nsa, w_out, b_out, ln_g, ln_b):
    depth = w_in.shape[0]
    alpha = float((2 * depth) ** 0.25)
    b, t, _ = x_prompt.shape
    bd, tn, _ = x_sample.shape
    ws, bs = _prepare_weights(w_in, b_in)
    xp = x_prompt.reshape(b * t, D_MODEL)
    xs = x_sample.reshape(bd * tn, D_MODEL)
    xpb, xsb = xp.astype(BF16), xs.astype(BF16)
    kv_p, kv_s, win_p, win_s = [], [], [], []
    for l in range(depth):
        shared = (_cmp_plane(cmp_w[l]), w_sb[l].astype(BF16), w_nsa[l].astype(BF16), w_out[l].astype(BF16),
                  b_out[l][None], ln_g[l][None], ln_b[l][None], alpha)
        xp, xpb, kvp, wp = _prompt_layer(xp, xpb, ws, bs, l, *shared, b, t)
        xs, xsb, kvs, wsm = _sample_layer(xs, xsb, ws, bs, l, page_table, cache_kv, state_win[l], *shared, bd, tn)
        kv_p.append(kvp)
        kv_s.append(kvs)
        win_p.append(wp)
        win_s.append(wsm)
    return (xp.reshape(b, t, D_MODEL), xs.reshape(bd, tn, D_MODEL), jnp.stack(kv_p), jnp.stack(kv_s),
            jnp.stack(win_p), jnp.stack(win_s))
```
